```python
import math, functools
import jax, jax.numpy as jnp
from jax import lax
import numpy as np

D_MODEL = 1024
BATCH = 8
SEQ = 2048
DEPTH = 1
DEC_BATCH = 32
DEC_SEQ = 1
PAST_LEN = 16384
PAGE_SIZE = 128

GDN_HEADS = 8
GDN_DK = 128
GDN_DV = 128
GDN_QK_W = GDN_HEADS * GDN_DK
GDN_V_W = GDN_HEADS * GDN_DV
CONV_K = 4
CONV_CH = 2 * GDN_QK_W + GDN_V_W
GDN_CHUNK = 64

MLA_HEADS = 8
Q_LORA = 384
KV_LORA = 256
NOPE = 128
ROPE = 64
V_DIM = 128
MLA_V_W = MLA_HEADS * V_DIM
ROPE_THETA = 10000.0
ATTN_SCALE = (NOPE + ROPE) ** -0.5
Q_BLOCK = 128
KV_BLOCK_PAGES = 4

PEER_HEADS = 8
N_KEYS = 128
N_EXPERTS = N_KEYS * N_KEYS
PEER_DKEY = 256
PEER_TOPK = 16
PEER_BLOCK = 128

EPS = 1e-6

IN_OFFSETS = (
    CONV_CH,
    CONV_CH + GDN_V_W,
    CONV_CH + GDN_V_W + GDN_HEADS,
    CONV_CH + GDN_V_W + 2 * GDN_HEADS,
    CONV_CH + GDN_V_W + 2 * GDN_HEADS + Q_LORA,
    CONV_CH + GDN_V_W + 2 * GDN_HEADS + Q_LORA + KV_LORA + ROPE,
)
IN_WIDTH = IN_OFFSETS[-1] + 2 * D_MODEL

kernel_name = 'hybrid_gdn_mla_peer_adaln_step'


def rms_norm(x, g):
    xf = x.astype(jnp.float32)
    y = xf * lax.rsqrt(jnp.mean(xf * xf, axis=-1, keepdims=True) + EPS)
    return (y * g.astype(jnp.float32)).astype(x.dtype)


def l2_normalize(x):
    xf = x.astype(jnp.float32)
    return xf * lax.rsqrt(jnp.sum(xf * xf, axis=-1, keepdims=True) + EPS)


def rotary(x, pos):
    half = ROPE // 2
    inv_freq = ROPE_THETA ** (-jnp.arange(half, dtype=jnp.float32) / half)
    ang = pos.astype(jnp.float32)[:, None] * inv_freq[None, :]
    cos = jnp.cos(ang)[None, :, None, :]
    sin = jnp.sin(ang)[None, :, None, :]
    xf = x.astype(jnp.float32)
    x1, x2 = xf[..., :half], xf[..., half:]
    return jnp.concatenate([x1 * cos - x2 * sin, x2 * cos + x1 * sin], axis=-1).astype(x.dtype)


def causal_conv(x, buf, w):
    s = x.shape[1]
    xp = jnp.concatenate([buf.astype(x.dtype), x], axis=1)
    y = xp[:, 0:s] * w[0]
    for j in range(1, CONV_K):
        y = y + xp[:, j:j + s] * w[j]
    return jax.nn.silu(y), xp[:, s:]


def gated_delta_rule(q, k, v, g, beta, s0):
    f32 = jnp.float32
    bsz, s, h, dk = q.shape
    dv = v.shape[-1]
    c = GDN_CHUNK
    pad = (-s) % c
    n = (s + pad) // c

    def chunks(t):
        t = jnp.pad(t.astype(f32), [(0, 0), (0, pad)] + [(0, 0)] * (t.ndim - 2))
        t = t.reshape((bsz, n, c) + t.shape[2:])
        return jnp.moveaxis(t, 3, 1)

    q, k, v, g, beta = map(chunks, (q, k, v, g, beta))
    gc = jnp.cumsum(g, axis=-1)
    incl = jnp.tril(jnp.ones((c, c), dtype=bool))
    strict = jnp.tril(jnp.ones((c, c), dtype=bool), -1)
    diff = gc[..., :, None] - gc[..., None, :]
    decay = jnp.where(incl, jnp.exp(jnp.where(incl, diff, 0.0)), 0.0)
    kb = k * beta[..., None]
    a_mat = jnp.where(strict, jnp.einsum('bhncd,bhnsd->bhncs', kb, k) * decay, 0.0)
    rhs = jnp.concatenate([v * beta[..., None], kb * jnp.exp(gc)[..., None]], axis=-1)
    sol = lax.linalg.triangular_solve(a_mat + jnp.eye(c, dtype=f32), rhs,
                                      left_side=True, lower=True, unit_diagonal=True)
    u, w = sol[..., :dv], sol[..., dv:]
    qk = jnp.where(incl, jnp.einsum('bhncd,bhnsd->bhncs', q, k) * decay, 0.0)

    def step(state, xs):
        qc, kc, uc, wc, gcc, qkc = xs
        v_new = uc - jnp.einsum('bhck,bhkv->bhcv', wc, state)
        o = (jnp.einsum('bhck,bhkv->bhcv', qc * jnp.exp(gcc)[..., None], state)
             + jnp.einsum('bhcs,bhsv->bhcv', qkc, v_new))
        g_last = gcc[..., -1]
        state = (state * jnp.exp(g_last)[..., None, None]
                 + jnp.einsum('bhck,bhcv->bhkv', kc * jnp.exp(g_last[..., None] - gcc)[..., None], v_new))
        return state, o

    xs = tuple(jnp.moveaxis(t, 2, 0) for t in (q, k, u, w, gc, qk))
    s_fin, o = lax.scan(step, s0.astype(f32), xs)
    o = jnp.transpose(o, (1, 0, 3, 2, 4)).reshape(bsz, n * c, h, dv)[:, :s]
    return o, s_fin.astype(s0.dtype)


def gdn_branch(qkv, z, a, b, conv_buf, s0, p):
    bsz, s = qkv.shape[:2]
    f32 = jnp.float32
    qkv_c, new_buf = causal_conv(qkv, conv_buf, p['conv_w'])
    q = qkv_c[..., :GDN_QK_W].reshape(bsz, s, GDN_HEADS, GDN_DK)
    k = qkv_c[..., GDN_QK_W:2 * GDN_QK_W].reshape(bsz, s, GDN_HEADS, GDN_DK)
    v = qkv_c[..., 2 * GDN_QK_W:].reshape(bsz, s, GDN_HEADS, GDN_DV)
    g = -jnp.exp(p['gdn_a_log'].astype(f32)) * jax.nn.softplus(a.astype(f32) + p['gdn_dt_bias'].astype(f32))
    beta = jax.nn.sigmoid(b.astype(f32))
    o, s_new = gated_delta_rule(l2_normalize(q) * GDN_DK ** -0.5, l2_normalize(k), v, g, beta, s0)
    o = rms_norm(o.astype(qkv.dtype), p['gdn_norm_g']) * jax.nn.silu(z.reshape(bsz, s, GDN_HEADS, GDN_DV))
    return o.reshape(bsz, s, GDN_V_W), new_buf, s_new


def mla_project(q_a, kv_a, pos, p):
    bsz, s = q_a.shape[:2]
    cq = rms_norm(q_a, p['q_a_norm_g'])
    q = jnp.einsum('bsc,chd->bshd', cq, p['w_qb'])
    q_nope = rms_norm(q[..., :NOPE], p['q_nope_norm_g'])
    q_rope = rotary(rms_norm(q[..., NOPE:], p['q_rope_norm_g']), pos)
    ckv = rms_norm(kv_a[..., :KV_LORA], p['kv_a_norm_g'])
    k_rope = rotary(rms_norm(kv_a[..., KV_LORA:], p['k_rope_norm_g'])[:, :, None, :], pos)[:, :, 0]
    return q_nope, q_rope, ckv, k_rope


def mla_expand(ckv, p):
    kv = jnp.einsum('bsc,chd->bshd', ckv, p['w_kvb'])
    return rms_norm(kv[..., :NOPE], p['k_nope_norm_g']), kv[..., NOPE:]


def mla_scores(q_nope, q_rope, k_nope, k_rope):
    s = jnp.einsum('bthd,bkhd->bhtk', q_nope, k_nope) + jnp.einsum('bthr,bkr->bhtk', q_rope, k_rope)
    return s.astype(jnp.float32) * ATTN_SCALE


def mla_prompt_attention(q_nope, q_rope, ckv, k_rope, p):
    bsz, s = q_nope.shape[:2]
    k_nope, v = mla_expand(ckv, p)
    qb = min(Q_BLOCK, s)
    nb = s // qb
    qn = q_nope.reshape(bsz, nb, qb, MLA_HEADS, NOPE).swapaxes(0, 1)
    qr = q_rope.reshape(bsz, nb, qb, MLA_HEADS, ROPE).swapaxes(0, 1)
    kpos = jnp.arange(s, dtype=jnp.int32)

    def block(args):
        i, qn_b, qr_b = args
        sc = mla_scores(qn_b, qr_b, k_nope, k_rope)
        qpos = i * qb + jnp.arange(qb, dtype=jnp.int32)
        sc = jnp.where(kpos[None, :] <= qpos[:, None], sc, -jnp.inf)
        pr = jax.nn.softmax(sc, axis=-1)
        return jnp.einsum('bhqk,bkhd->bqhd', pr.astype(v.dtype), v)

    o = lax.map(block, (jnp.arange(nb, dtype=jnp.int32), qn, qr))
    return o.swapaxes(0, 1).reshape(bsz, s, MLA_V_W)


def mla_sample_attention(q_nope, q_rope, ckv, k_rope, p, cache_latent, cache_krope, page_table):
    f32 = jnp.float32
    bsz, t = q_nope.shape[:2]
    n_pages = page_table.shape[1]
    pb = math.gcd(n_pages, KV_BLOCK_PAGES)
    blocks = page_table.reshape(bsz, n_pages // pb, pb).swapaxes(0, 1)

    def merge(carry, sc, v):
        m, l, acc = carry
        m_new = jnp.maximum(m, jnp.max(sc, axis=-1))
        e = jnp.exp(sc - m_new[..., None])
        corr = jnp.exp(m - m_new)
        acc = acc * corr[..., None] + jnp.einsum('bhtk,bkhd->bhtd', e, v.astype(f32))
        return (m_new, l * corr + jnp.sum(e, axis=-1), acc)

    def step(carry, pages):
        lat = cache_latent[pages].reshape(bsz, pb * PAGE_SIZE, KV_LORA)
        kr = cache_krope[pages].reshape(bsz, pb * PAGE_SIZE, ROPE)
        k_nope, v = mla_expand(lat, p)
        return merge(carry, mla_scores(q_nope, q_rope, k_nope, kr), v), None

    init = (jnp.full((bsz, MLA_HEADS, t), -jnp.inf, f32),
            jnp.zeros((bsz, MLA_HEADS, t), f32),
            jnp.zeros((bsz, MLA_HEADS, t, V_DIM), f32))
    carry, _ = lax.scan(step, init, blocks)
    k_nope_n, v_n = mla_expand(ckv, p)
    sc = mla_scores(q_nope, q_rope, k_nope_n, k_rope)
    sc = jnp.where(jnp.tril(jnp.ones((t, t), dtype=bool)), sc, -jnp.inf)
    m, l, acc = merge(carry, sc, v_n)
    o = acc / l[..., None]
    return jnp.transpose(o, (0, 2, 1, 3)).reshape(bsz, t, MLA_V_W).astype(q_nope.dtype)


def peer_ffn(x, p):
    n_tok, d = x.shape
    pad = (-n_tok) % PEER_BLOCK
    xb = jnp.pad(x, ((0, pad), (0, 0))).reshape(-1, PEER_BLOCK, d)

    def block(xt):
        q = (xt @ p['peer_wq']).reshape(PEER_BLOCK, PEER_HEADS, 2, PEER_DKEY // 2)
        s = jnp.einsum('thcd,hcnd->thcn', q, p['peer_keys']).astype(jnp.float32)
        s1, i1 = lax.top_k(s[:, :, 0], PEER_TOPK)
        s2, i2 = lax.top_k(s[:, :, 1], PEER_TOPK)
        cand_s = (s1[..., :, None] + s2[..., None, :]).reshape(PEER_BLOCK, PEER_HEADS, PEER_TOPK * PEER_TOPK)
        cand_i = (i1[..., :, None] * N_KEYS + i2[..., None, :]).reshape(PEER_BLOCK, PEER_HEADS, PEER_TOPK * PEER_TOPK)
        top_s, j = lax.top_k(cand_s, PEER_TOPK)
        idx = jnp.take_along_axis(cand_i, j, axis=-1)
        gate = jax.nn.softmax(top_s, axis=-1)
        u = p['peer_u'][idx]
        act = jax.nn.gelu(jnp.einsum('thkd,td->thk', u, xt).astype(jnp.float32), approximate=False)
        return jnp.einsum('thk,thkd->td', (gate * act).astype(xt.dtype), p['peer_v'][idx])

    return lax.map(block, xb).reshape(-1, d)[:n_tok]


def trunk_layer(x, c, pos, conv_buf, gdn_state, attend, p):
    bsz, s, d = x.shape
    mod = jax.nn.silu(c) @ p['w_ada'] + p['b_ada']
    sh1, sc1, g1, sh2, sc2, g2 = jnp.split(mod[:, None, :], 6, axis=-1)
    h = rms_norm(x, p['norm1_g']) * (1 + sc1) + sh1
    qkv, z, a, b, q_a, kv_a, gates = jnp.split(h @ p['w_in'], IN_OFFSETS, axis=-1)
    o_a, new_conv, new_s = gdn_branch(qkv, z, a, b, conv_buf, gdn_state, p)
    q_nope, q_rope, ckv, k_rope = mla_project(q_a, kv_a, pos, p)
    o_b = attend(q_nope, q_rope, ckv, k_rope, p)
    gate_a, gate_b = jnp.split(jax.nn.sigmoid(gates), 2, axis=-1)
    merged = gate_a * (o_a @ p['w_branch_a']) + gate_b * (o_b @ p['w_branch_b'])
    x = x + g1 * (merged @ p['w_out'])
    h2 = rms_norm(x, p['norm2_g']) * (1 + sc2) + sh2
    x = x + g2 * peer_ffn(h2.reshape(bsz * s, d), p).reshape(bsz, s, d)
    return x, ckv, k_rope, new_s, new_conv


def setup_inputs(seed: int = 0) -> dict:
    key = jax.random.key(seed)
    keys = iter(jax.random.split(key, 48))
    f32 = jnp.float32

    def nrm(shape, scale):
        return jax.random.normal(next(keys), shape, f32) * scale

    def gain(n):
        return 1.0 + nrm((DEPTH, n), 0.02)

    n_pages = PAST_LEN // PAGE_SIZE
    n_used = DEC_BATCH * n_pages
    n_pool = n_used + max(1, n_used // 4)
    x_prompt = nrm((BATCH, SEQ, D_MODEL), 1.0)
    x_sample = nrm((DEC_BATCH, DEC_SEQ, D_MODEL), 1.0)
    c_prompt = nrm((BATCH, D_MODEL), 1.0)
    c_sample = nrm((DEC_BATCH, D_MODEL), 1.0)
    cache_latent = nrm((DEPTH, n_pool, PAGE_SIZE, KV_LORA), 1.0)
    cache_krope = nrm((DEPTH, n_pool, PAGE_SIZE, ROPE), 1.0)
    state_gdn = nrm((DEPTH, DEC_BATCH, GDN_HEADS, GDN_DK, GDN_DV), 0.1)
    state_conv = nrm((DEPTH, DEC_BATCH, CONV_K - 1, CONV_CH), 1.0)
    page_table = jax.random.permutation(next(keys), n_pool)[:n_used].reshape(DEC_BATCH, n_pages).astype(jnp.int32)
    a_log = jnp.log(jax.random.uniform(next(keys), (DEPTH, GDN_HEADS), f32, 1.0, 16.0))
    dt = jnp.exp(jax.random.uniform(next(keys), (DEPTH, GDN_HEADS), f32, math.log(1e-3), math.log(1e-1)))
    dt_bias = dt + jnp.log(-jnp.expm1(-dt))
    return {
        'x_prompt': x_prompt,
        'x_sample': x_sample,
        'c_prompt': c_prompt,
        'c_sample': c_sample,
        'cache_latent': cache_latent,
        'cache_krope': cache_krope,
        'state_gdn': state_gdn,
        'state_conv': state_conv,
        'page_table': page_table,
        'w_ada': nrm((DEPTH, D_MODEL, 6 * D_MODEL), D_MODEL ** -0.5),
        'b_ada': nrm((DEPTH, 6 * D_MODEL), 0.02),
        'norm1_g': gain(D_MODEL),
        'norm2_g': gain(D_MODEL),
        'w_in': nrm((DEPTH, D_MODEL, IN_WIDTH), D_MODEL ** -0.5),
        'conv_w': nrm((DEPTH, CONV_K, CONV_CH), 0.5),
        'gdn_a_log': a_log,
        'gdn_dt_bias': dt_bias,
        'gdn_norm_g': gain(GDN_DV),
        'q_a_norm_g': gain(Q_LORA),
        'w_qb': nrm((DEPTH, Q_LORA, MLA_HEADS, NOPE + ROPE), Q_LORA ** -0.5),
        'kv_a_norm_g': gain(KV_LORA),
        'w_kvb': nrm((DEPTH, KV_LORA, MLA_HEADS, NOPE + V_DIM), KV_LORA ** -0.5),
        'q_nope_norm_g': gain(NOPE),
        'q_rope_norm_g': gain(ROPE),
        'k_nope_norm_g': gain(NOPE),
        'k_rope_norm_g': gain(ROPE),
        'w_branch_a': nrm((DEPTH, GDN_V_W, D_MODEL), GDN_V_W ** -0.5),
        'w_branch_b': nrm((DEPTH, MLA_V_W, D_MODEL), MLA_V_W ** -0.5),
        'w_out': nrm((DEPTH, D_MODEL, D_MODEL), D_MODEL ** -0.5),
        'peer_wq': nrm((DEPTH, D_MODEL, PEER_HEADS * PEER_DKEY), D_MODEL ** -0.5),
        'peer_keys': nrm((DEPTH, PEER_HEADS, 2, N_KEYS, PEER_DKEY // 2), (PEER_DKEY // 2) ** -0.5),
        'peer_u': nrm((DEPTH, N_EXPERTS, D_MODEL), D_MODEL ** -0.5),
        'peer_v': nrm((DEPTH, N_EXPERTS, D_MODEL), PEER_HEADS ** -0.5),
    }


def reference(x_prompt, x_sample, c_prompt, c_sample, cache_latent, cache_krope, state_gdn, state_conv,
              page_table, w_ada, b_ada, norm1_g, norm2_g, w_in, conv_w, gdn_a_log, gdn_dt_bias, gdn_norm_g,
              q_a_norm_g, w_qb, kv_a_norm_g, w_kvb, q_nope_norm_g, q_rope_norm_g, k_nope_norm_g, k_rope_norm_g,
              w_branch_a, w_branch_b, w_out, peer_wq, peer_keys, peer_u, peer_v):
    bsz_p, seq_p = x_prompt.shape[:2]
    pos_p = jnp.arange(seq_p, dtype=jnp.int32)
    past = page_table.shape[1] * PAGE_SIZE
    pos_s = past + jnp.arange(x_sample.shape[1], dtype=jnp.int32)
    xp, xs = x_prompt, x_sample
    p_lat, p_kr, p_gdn, p_conv = [], [], [], []
    s_lat, s_kr, s_gdn, s_conv = [], [], [], []
    for l in range(DEPTH):
        p = {
            'w_ada': w_ada[l], 'b_ada': b_ada[l], 'norm1_g': norm1_g[l], 'norm2_g': norm2_g[l],
            'w_in': w_in[l], 'conv_w': conv_w[l], 'gdn_a_log': gdn_a_log[l], 'gdn_dt_bias': gdn_dt_bias[l],
            'gdn_norm_g': gdn_norm_g[l], 'q_a_norm_g': q_a_norm_g[l], 'w_qb': w_qb[l],
            'kv_a_norm_g': kv_a_norm_g[l], 'w_kvb': w_kvb[l], 'q_nope_norm_g': q_nope_norm_g[l],
            'q_rope_norm_g': q_rope_norm_g[l], 'k_nope_norm_g': k_nope_norm_g[l],
            'k_rope_norm_g': k_rope_norm_g[l], 'w_branch_a': w_branch_a[l], 'w_branch_b': w_branch_b[l],
            'w_out': w_out[l], 'peer_wq': peer_wq[l], 'peer_keys': peer_keys[l], 'peer_u': peer_u[l],
            'peer_v': peer_v[l],
        }
        conv0 = jnp.zeros((bsz_p, CONV_K - 1, CONV_CH), x_prompt.dtype)
        gdn0 = jnp.zeros((bsz_p, GDN_HEADS, GDN_DK, GDN_DV), x_prompt.dtype)
        xp, lat_p, kr_p, g_p, cv_p = trunk_layer(xp, c_prompt, pos_p, conv0, gdn0, mla_prompt_attention, p)
        attend = functools.partial(mla_sample_attention, cache_latent=cache_latent[l],
                                   cache_krope=cache_krope[l], page_table=page_table)
        xs, lat_s, kr_s, g_s, cv_s = trunk_layer(xs, c_sample, pos_s, state_conv[l], state_gdn[l], attend, p)
        p_lat.append(lat_p); p_kr.append(kr_p); p_gdn.append(g_p); p_conv.append(cv_p)
        s_lat.append(lat_s); s_kr.append(kr_s); s_gdn.append(g_s); s_conv.append(cv_s)
    return (xp, xs,
            jnp.stack(p_lat), jnp.stack(p_kr), jnp.stack(p_gdn), jnp.stack(p_conv),
            jnp.stack(s_lat), jnp.stack(s_kr), jnp.stack(s_gdn), jnp.stack(s_conv))
```

```python
import functools
import math

import jax
import jax.numpy as jnp
from jax import lax
from jax.experimental import pallas as pl
from jax.experimental.pallas import tpu as pltpu

D_MODEL = 1024
PAGE_SIZE = 128
GDN_HEADS = 8
GDN_DK = 128
GDN_DV = 128
GDN_QK_W = GDN_HEADS * GDN_DK
GDN_V_W = GDN_HEADS * GDN_DV
CONV_K = 4
CONV_CH = 2 * GDN_QK_W + GDN_V_W
GDN_CHUNK = 64
MLA_HEADS = 8
Q_LORA = 384
KV_LORA = 256
NOPE = 128
ROPE = 64
V_DIM = 128
MLA_V_W = MLA_HEADS * V_DIM
ROPE_THETA = 10000.0
ATTN_SCALE = (NOPE + ROPE) ** -0.5
Q_BLOCK = 128
KV_BLOCK_PAGES = 4
PEER_HEADS = 8
N_KEYS = 128
PEER_DKEY = 256
PEER_TOPK = 16
PEER_BLOCK = 128
EPS = 1e-6
IN_OFFSETS = (
    CONV_CH,
    CONV_CH + GDN_V_W,
    CONV_CH + GDN_V_W + GDN_HEADS,
    CONV_CH + GDN_V_W + 2 * GDN_HEADS,
    CONV_CH + GDN_V_W + 2 * GDN_HEADS + Q_LORA,
    CONV_CH + GDN_V_W + 2 * GDN_HEADS + Q_LORA + KV_LORA + ROPE,
)

LANES = 128
MXU_DIM = 256


def _round_up(x, m):
    return (x + m - 1) // m * m


def _matmul_body(x_ref, w_ref, o_ref):
    o_ref[...] = jnp.dot(x_ref[...].astype(jnp.bfloat16), w_ref[...],
                         preferred_element_type=jnp.float32)


def _matmul(x, w, *, tm=512, tn=512):
    m, k = x.shape
    n = w.shape[1]
    n_pad = _round_up(n, tn)
    m_pad = _round_up(m, 8)
    tm = min(tm, m_pad)
    m_pad = _round_up(m_pad, tm)
    wb = w.astype(jnp.bfloat16)
    if n_pad != n:
        wb = jnp.pad(wb, ((0, 0), (0, n_pad - n)))
    if m_pad != m:
        x = jnp.pad(x, ((0, m_pad - m), (0, 0)))
    out = pl.pallas_call(
        _matmul_body,
        grid=(m_pad // tm, n_pad // tn),
        in_specs=[pl.BlockSpec((tm, k), lambda i, j: (i, 0)),
                  pl.BlockSpec((k, tn), lambda i, j: (0, j))],
        out_specs=pl.BlockSpec((tm, tn), lambda i, j: (i, j)),
        out_shape=jax.ShapeDtypeStruct((m_pad, n_pad), jnp.float32),
        compiler_params=pltpu.CompilerParams(
            dimension_semantics=("parallel", "parallel")),
        name="dense_matmul",
    )(x, wb)
    return out[:m, :n]


def rms_norm(x, g):
    xf = x.astype(jnp.float32)
    y = xf * lax.rsqrt(jnp.mean(xf * xf, axis=-1, keepdims=True) + EPS)
    return (y * g.astype(jnp.float32)).astype(x.dtype)


def l2_normalize(x):
    xf = x.astype(jnp.float32)
    return xf * lax.rsqrt(jnp.sum(xf * xf, axis=-1, keepdims=True) + EPS)


def rotary(x, pos):
    half = ROPE // 2
    inv_freq = ROPE_THETA ** (-jnp.arange(half, dtype=jnp.float32) / half)
    ang = pos.astype(jnp.float32)[:, None] * inv_freq[None, :]
    cos = jnp.cos(ang)[None, :, None, :]
    sin = jnp.sin(ang)[None, :, None, :]
    xf = x.astype(jnp.float32)
    x1, x2 = xf[..., :half], xf[..., half:]
    return jnp.concatenate([x1 * cos - x2 * sin, x2 * cos + x1 * sin], axis=-1).astype(x.dtype)


def causal_conv(x, buf, w):
    s = x.shape[1]
    xp = jnp.concatenate([buf.astype(x.dtype), x], axis=1)
    y = xp[:, 0:s] * w[0]
    for j in range(1, CONV_K):
        y = y + xp[:, j:j + s] * w[j]
    return jax.nn.silu(y), xp[:, s:]


def gated_delta_rule(q, k, v, g, beta, s0):
    f32 = jnp.float32
    bsz, s, h, dk = q.shape
    dv = v.shape[-1]
    c = GDN_CHUNK
    pad = (-s) % c
    n = (s + pad) // c

    def chunks(t):
        t = jnp.pad(t.astype(f32), [(0, 0), (0, pad)] + [(0, 0)] * (t.ndim - 2))
        t = t.reshape((bsz, n, c) + t.shape[2:])
        return jnp.moveaxis(t, 3, 1)

    q, k, v, g, beta = map(chunks, (q, k, v, g, beta))
    gc = jnp.cumsum(g, axis=-1)
    incl = jnp.tril(jnp.ones((c, c), dtype=bool))
    strict = jnp.tril(jnp.ones((c, c), dtype=bool), -1)
    diff = gc[..., :, None] - gc[..., None, :]
    decay = jnp.where(incl, jnp.exp(jnp.where(incl, diff, 0.0)), 0.0)
    kb = k * beta[..., None]
    a_mat = jnp.where(strict, jnp.einsum('bhncd,bhnsd->bhncs', kb, k) * decay, 0.0)
    rhs = jnp.concatenate([v * beta[..., None], kb * jnp.exp(gc)[..., None]], axis=-1)
    sol = lax.linalg.triangular_solve(a_mat + jnp.eye(c, dtype=f32), rhs,
                                      left_side=True, lower=True, unit_diagonal=True)
    u, w = sol[..., :dv], sol[..., dv:]
    qk = jnp.where(incl, jnp.einsum('bhncd,bhnsd->bhncs', q, k) * decay, 0.0)

    def step(state, xs):
        qc, kc, uc, wc, gcc, qkc = xs
        v_new = uc - jnp.einsum('bhck,bhkv->bhcv', wc, state)
        o = (jnp.einsum('bhck,bhkv->bhcv', qc * jnp.exp(gcc)[..., None], state)
             + jnp.einsum('bhcs,bhsv->bhcv', qkc, v_new))
        g_last = gcc[..., -1]
        state = (state * jnp.exp(g_last)[..., None, None]
                 + jnp.einsum('bhck,bhcv->bhkv', kc * jnp.exp(g_last[..., None] - gcc)[..., None], v_new))
        return state, o

    xs = tuple(jnp.moveaxis(t, 2, 0) for t in (q, k, u, w, gc, qk))
    s_fin, o = lax.scan(step, s0.astype(f32), xs)
    o = jnp.transpose(o, (1, 0, 3, 2, 4)).reshape(bsz, n * c, h, dv)[:, :s]
    return o, s_fin.astype(s0.dtype)


def gdn_branch(qkv, z, a, b, conv_buf, s0, p):
    bsz, s = qkv.shape[:2]
    f32 = jnp.float32
    qkv_c, new_buf = causal_conv(qkv, conv_buf, p['conv_w'])
    q = qkv_c[..., :GDN_QK_W].reshape(bsz, s, GDN_HEADS, GDN_DK)
    k = qkv_c[..., GDN_QK_W:2 * GDN_QK_W].reshape(bsz, s, GDN_HEADS, GDN_DK)
    v = qkv_c[..., 2 * GDN_QK_W:].reshape(bsz, s, GDN_HEADS, GDN_DV)
    g = -jnp.exp(p['gdn_a_log'].astype(f32)) * jax.nn.softplus(a.astype(f32) + p['gdn_dt_bias'].astype(f32))
    beta = jax.nn.sigmoid(b.astype(f32))
    o, s_new = gated_delta_rule(l2_normalize(q) * GDN_DK ** -0.5, l2_normalize(k), v, g, beta, s0)
    o = rms_norm(o.astype(qkv.dtype), p['gdn_norm_g']) * jax.nn.silu(z.reshape(bsz, s, GDN_HEADS, GDN_DV))
    return o.reshape(bsz, s, GDN_V_W), new_buf, s_new


def mla_project(q_a, kv_a, pos, p):
    bsz, s = q_a.shape[:2]
    cq = rms_norm(q_a, p['q_a_norm_g'])
    q = _matmul(cq.reshape(bsz * s, Q_LORA), p['w_qb'].reshape(Q_LORA, -1)).reshape(
        bsz, s, MLA_HEADS, NOPE + ROPE)
    q_nope = rms_norm(q[..., :NOPE], p['q_nope_norm_g'])
    q_rope = rotary(rms_norm(q[..., NOPE:], p['q_rope_norm_g']), pos)
    ckv = rms_norm(kv_a[..., :KV_LORA], p['kv_a_norm_g'])
    k_rope = rotary(rms_norm(kv_a[..., KV_LORA:], p['k_rope_norm_g'])[:, :, None, :], pos)[:, :, 0]
    return q_nope, q_rope, ckv, k_rope


def mla_expand(ckv, p):
    kv = jnp.einsum('bsc,chd->bshd', ckv, p['w_kvb'])
    return rms_norm(kv[..., :NOPE], p['k_nope_norm_g']), kv[..., NOPE:]


def mla_scores(q_nope, q_rope, k_nope, k_rope):
    s = jnp.einsum('bthd,bkhd->bhtk', q_nope, k_nope) + jnp.einsum('bthr,bkr->bhtk', q_rope, k_rope)
    return s.astype(jnp.float32) * ATTN_SCALE


def mla_prompt_attention(q_nope, q_rope, ckv, k_rope, p):
    bsz, s = q_nope.shape[:2]
    k_nope, v = mla_expand(ckv, p)
    qb = min(Q_BLOCK, s)
    nb = s // qb
    qn = q_nope.reshape(bsz, nb, qb, MLA_HEADS, NOPE).swapaxes(0, 1)
    qr = q_rope.reshape(bsz, nb, qb, MLA_HEADS, ROPE).swapaxes(0, 1)
    kpos = jnp.arange(s, dtype=jnp.int32)

    def block(args):
        i, qn_b, qr_b = args
        sc = mla_scores(qn_b, qr_b, k_nope, k_rope)
        qpos = i * qb + jnp.arange(qb, dtype=jnp.int32)
        sc = jnp.where(kpos[None, :] <= qpos[:, None], sc, -jnp.inf)
        pr = jax.nn.softmax(sc, axis=-1)
        return jnp.einsum('bhqk,bkhd->bqhd', pr.astype(v.dtype), v)

    o = lax.map(block, (jnp.arange(nb, dtype=jnp.int32), qn, qr))
    return o.swapaxes(0, 1).reshape(bsz, s, MLA_V_W)


def mla_sample_attention(q_nope, q_rope, ckv, k_rope, p, cache_latent, cache_krope, page_table):
    f32 = jnp.float32
    bsz, t = q_nope.shape[:2]
    n_pages = page_table.shape[1]
    pb = math.gcd(n_pages, KV_BLOCK_PAGES)
    blocks = page_table.reshape(bsz, n_pages // pb, pb).swapaxes(0, 1)

    def merge(carry, sc, v):
        m, l, acc = carry
        m_new = jnp.maximum(m, jnp.max(sc, axis=-1))
        e = jnp.exp(sc - m_new[..., None])
        corr = jnp.exp(m - m_new)
        acc = acc * corr[..., None] + jnp.einsum('bhtk,bkhd->bhtd', e, v.astype(f32))
        return (m_new, l * corr + jnp.sum(e, axis=-1), acc)

    def step(carry, pages):
        lat = cache_latent[pages].reshape(bsz, pb * PAGE_SIZE, KV_LORA)
        kr = cache_krope[pages].reshape(bsz, pb * PAGE_SIZE, ROPE)
        k_nope, v = mla_expand(lat, p)
        return merge(carry, mla_scores(q_nope, q_rope, k_nope, kr), v), None

    init = (jnp.full((bsz, MLA_HEADS, t), -jnp.inf, f32),
            jnp.zeros((bsz, MLA_HEADS, t), f32),
            jnp.zeros((bsz, MLA_HEADS, t, V_DIM), f32))
    carry, _ = lax.scan(step, init, blocks)
    k_nope_n, v_n = mla_expand(ckv, p)
    sc = mla_scores(q_nope, q_rope, k_nope_n, k_rope)
    sc = jnp.where(jnp.tril(jnp.ones((t, t), dtype=bool)), sc, -jnp.inf)
    m, l, acc = merge(carry, sc, v_n)
    o = acc / l[..., None]
    return jnp.transpose(o, (0, 2, 1, 3)).reshape(bsz, t, MLA_V_W).astype(q_nope.dtype)


def peer_ffn(x, p):
    n_tok, d = x.shape
    pad = (-n_tok) % PEER_BLOCK
    xq = _matmul(x, p['peer_wq'])
    xb = jnp.pad(x, ((0, pad), (0, 0))).reshape(-1, PEER_BLOCK, d)
    qb = jnp.pad(xq, ((0, pad), (0, 0))).reshape(-1, PEER_BLOCK, PEER_HEADS * PEER_DKEY)

    def block(args):
        xt, qt = args
        q = qt.reshape(PEER_BLOCK, PEER_HEADS, 2, PEER_DKEY // 2)
        s = jnp.einsum('thcd,hcnd->thcn', q, p['peer_keys']).astype(jnp.float32)
        s1, i1 = lax.top_k(s[:, :, 0], PEER_TOPK)
        s2, i2 = lax.top_k(s[:, :, 1], PEER_TOPK)
        cand_s = (s1[..., :, None] + s2[..., None, :]).reshape(PEER_BLOCK, PEER_HEADS, PEER_TOPK * PEER_TOPK)
        cand_i = (i1[..., :, None] * N_KEYS + i2[..., None, :]).reshape(PEER_BLOCK, PEER_HEADS, PEER_TOPK * PEER_TOPK)
        top_s, j = lax.top_k(cand_s, PEER_TOPK)
        idx = jnp.take_along_axis(cand_i, j, axis=-1)
        gate = jax.nn.softmax(top_s, axis=-1)
        u = p['peer_u'][idx]
        act = jax.nn.gelu(jnp.einsum('thkd,td->thk', u, xt).astype(jnp.float32), approximate=False)
        return jnp.einsum('thk,thkd->td', (gate * act).astype(xt.dtype), p['peer_v'][idx])

    return lax.map(block, (xb, qb)).reshape(-1, d)[:n_tok]


def trunk_layer(x, c, pos, conv_buf, gdn_state, attend, p):
    bsz, s, d = x.shape
    mod = jax.nn.silu(c) @ p['w_ada'] + p['b_ada']
    sh1, sc1, g1, sh2, sc2, g2 = jnp.split(mod[:, None, :], 6, axis=-1)
    h = rms_norm(x, p['norm1_g']) * (1 + sc1) + sh1
    proj = _matmul(h.reshape(bsz * s, d), p['w_in']).reshape(bsz, s, -1)
    qkv, z, a, b, q_a, kv_a, gates = jnp.split(proj, IN_OFFSETS, axis=-1)
    o_a, new_conv, new_s = gdn_branch(qkv, z, a, b, conv_buf, gdn_state, p)
    q_nope, q_rope, ckv, k_rope = mla_project(q_a, kv_a, pos, p)
    o_b = attend(q_nope, q_rope, ckv, k_rope, p)
    gate_a, gate_b = jnp.split(jax.nn.sigmoid(gates), 2, axis=-1)
    ma = _matmul(o_a.reshape(bsz * s, -1), p['w_branch_a']).reshape(bsz, s, d)
    mb = _matmul(o_b.reshape(bsz * s, -1), p['w_branch_b']).reshape(bsz, s, d)
    merged = gate_a * ma + gate_b * mb
    x = x + g1 * _matmul(merged.reshape(bsz * s, d), p['w_out']).reshape(bsz, s, d)
    h2 = rms_norm(x, p['norm2_g']) * (1 + sc2) + sh2
    x = x + g2 * peer_ffn(h2.reshape(bsz * s, d), p).reshape(bsz, s, d)
    return x, ckv, k_rope, new_s, new_conv


def kernel(x_prompt, x_sample, c_prompt, c_sample, cache_latent, cache_krope, state_gdn, state_conv,
           page_table, w_ada, b_ada, norm1_g, norm2_g, w_in, conv_w, gdn_a_log, gdn_dt_bias, gdn_norm_g,
           q_a_norm_g, w_qb, kv_a_norm_g, w_kvb, q_nope_norm_g, q_rope_norm_g, k_nope_norm_g, k_rope_norm_g,
           w_branch_a, w_branch_b, w_out, peer_wq, peer_keys, peer_u, peer_v):
    bsz_p, seq_p = x_prompt.shape[:2]
    pos_p = jnp.arange(seq_p, dtype=jnp.int32)
    past = page_table.shape[1] * PAGE_SIZE
    pos_s = past + jnp.arange(x_sample.shape[1], dtype=jnp.int32)
    l = 0
    p = {
        'w_ada': w_ada[l], 'b_ada': b_ada[l], 'norm1_g': norm1_g[l], 'norm2_g': norm2_g[l],
        'w_in': w_in[l], 'conv_w': conv_w[l], 'gdn_a_log': gdn_a_log[l], 'gdn_dt_bias': gdn_dt_bias[l],
        'gdn_norm_g': gdn_norm_g[l], 'q_a_norm_g': q_a_norm_g[l], 'w_qb': w_qb[l],
        'kv_a_norm_g': kv_a_norm_g[l], 'w_kvb': w_kvb[l], 'q_nope_norm_g': q_nope_norm_g[l],
        'q_rope_norm_g': q_rope_norm_g[l], 'k_nope_norm_g': k_nope_norm_g[l],
        'k_rope_norm_g': k_rope_norm_g[l], 'w_branch_a': w_branch_a[l], 'w_branch_b': w_branch_b[l],
        'w_out': w_out[l], 'peer_wq': peer_wq[l], 'peer_keys': peer_keys[l], 'peer_u': peer_u[l],
        'peer_v': peer_v[l],
    }
    conv0 = jnp.zeros((bsz_p, CONV_K - 1, CONV_CH), x_prompt.dtype)
    gdn0 = jnp.zeros((bsz_p, GDN_HEADS, GDN_DK, GDN_DV), x_prompt.dtype)
    xp, lat_p, kr_p, g_p, cv_p = trunk_layer(x_prompt, c_prompt, pos_p, conv0, gdn0, mla_prompt_attention, p)
    attend = functools.partial(mla_sample_attention, cache_latent=cache_latent[l],
                               cache_krope=cache_krope[l], page_table=page_table)
    xs, lat_s, kr_s, g_s, cv_s = trunk_layer(x_sample, c_sample, pos_s, state_conv[l], state_gdn[l], attend, p)
    st = lambda t: t[None]
    return (xp, xs, st(lat_p), st(kr_p), st(g_p), st(cv_p), st(lat_s), st(kr_s), st(g_s), st(cv_s))
```

```python
import functools
import math

import jax
import jax.numpy as jnp
from jax import lax
from jax.experimental import pallas as pl
from jax.experimental.pallas import tpu as pltpu

D_MODEL = 1024
PAGE_SIZE = 128
GDN_HEADS = 8
GDN_DK = 128
GDN_DV = 128
GDN_QK_W = GDN_HEADS * GDN_DK
GDN_V_W = GDN_HEADS * GDN_DV
CONV_K = 4
CONV_CH = 2 * GDN_QK_W + GDN_V_W
GDN_CHUNK = 64
MLA_HEADS = 8
Q_LORA = 384
KV_LORA = 256
NOPE = 128
ROPE = 64
V_DIM = 128
MLA_V_W = MLA_HEADS * V_DIM
ROPE_THETA = 10000.0
ATTN_SCALE = (NOPE + ROPE) ** -0.5
Q_BLOCK = 128
KV_BLOCK_PAGES = 4
PEER_HEADS = 8
N_KEYS = 128
N_EXPERTS = N_KEYS * N_KEYS
PEER_DKEY = 256
PEER_TOPK = 16
PEER_BLOCK = 128
EPS = 1e-6
IN_OFFSETS = (
    CONV_CH,
    CONV_CH + GDN_V_W,
    CONV_CH + GDN_V_W + GDN_HEADS,
    CONV_CH + GDN_V_W + 2 * GDN_HEADS,
    CONV_CH + GDN_V_W + 2 * GDN_HEADS + Q_LORA,
    CONV_CH + GDN_V_W + 2 * GDN_HEADS + Q_LORA + KV_LORA + ROPE,
)

LANES = 128
MXU_DIM = 256


def _round_up(x, m):
    return (x + m - 1) // m * m


def _matmul_body(x_ref, w_ref, o_ref):
    o_ref[...] = jnp.dot(x_ref[...].astype(jnp.bfloat16), w_ref[...],
                         preferred_element_type=jnp.float32)


def _matmul(x, w, *, tm=512, tn=512):
    m, k = x.shape
    n = w.shape[1]
    n_pad = _round_up(n, tn)
    m_pad = _round_up(m, 8)
    tm = min(tm, m_pad)
    m_pad = _round_up(m_pad, tm)
    wb = w.astype(jnp.bfloat16)
    if n_pad != n:
        wb = jnp.pad(wb, ((0, 0), (0, n_pad - n)))
    if m_pad != m:
        x = jnp.pad(x, ((0, m_pad - m), (0, 0)))
    out = pl.pallas_call(
        _matmul_body,
        grid=(m_pad // tm, n_pad // tn),
        in_specs=[pl.BlockSpec((tm, k), lambda i, j: (i, 0)),
                  pl.BlockSpec((k, tn), lambda i, j: (0, j))],
        out_specs=pl.BlockSpec((tm, tn), lambda i, j: (i, j)),
        out_shape=jax.ShapeDtypeStruct((m_pad, n_pad), jnp.float32),
        compiler_params=pltpu.CompilerParams(
            dimension_semantics=("parallel", "parallel")),
        name="dense_matmul",
    )(x, wb)
    return out[:m, :n]


def rms_norm(x, g):
    xf = x.astype(jnp.float32)
    y = xf * lax.rsqrt(jnp.mean(xf * xf, axis=-1, keepdims=True) + EPS)
    return (y * g.astype(jnp.float32)).astype(x.dtype)


def l2_normalize(x):
    xf = x.astype(jnp.float32)
    return xf * lax.rsqrt(jnp.sum(xf * xf, axis=-1, keepdims=True) + EPS)


def rotary(x, pos):
    half = ROPE // 2
    inv_freq = ROPE_THETA ** (-jnp.arange(half, dtype=jnp.float32) / half)
    ang = pos.astype(jnp.float32)[:, None] * inv_freq[None, :]
    cos = jnp.cos(ang)[None, :, None, :]
    sin = jnp.sin(ang)[None, :, None, :]
    xf = x.astype(jnp.float32)
    x1, x2 = xf[..., :half], xf[..., half:]
    return jnp.concatenate([x1 * cos - x2 * sin, x2 * cos + x1 * sin], axis=-1).astype(x.dtype)


def causal_conv(x, buf, w):
    s = x.shape[1]
    xp = jnp.concatenate([buf.astype(x.dtype), x], axis=1)
    y = xp[:, 0:s] * w[0]
    for j in range(1, CONV_K):
        y = y + xp[:, j:j + s] * w[j]
    return jax.nn.silu(y), xp[:, s:]


def gated_delta_rule(q, k, v, g, beta, s0):
    f32 = jnp.float32
    bsz, s, h, dk = q.shape
    dv = v.shape[-1]
    c = GDN_CHUNK
    pad = (-s) % c
    n = (s + pad) // c

    def chunks(t):
        t = jnp.pad(t.astype(f32), [(0, 0), (0, pad)] + [(0, 0)] * (t.ndim - 2))
        t = t.reshape((bsz, n, c) + t.shape[2:])
        return jnp.moveaxis(t, 3, 1)

    q, k, v, g, beta = map(chunks, (q, k, v, g, beta))
    gc = jnp.cumsum(g, axis=-1)
    incl = jnp.tril(jnp.ones((c, c), dtype=bool))
    strict = jnp.tril(jnp.ones((c, c), dtype=bool), -1)
    diff = gc[..., :, None] - gc[..., None, :]
    decay = jnp.where(incl, jnp.exp(jnp.where(incl, diff, 0.0)), 0.0)
    kb = k * beta[..., None]
    a_mat = jnp.where(strict, jnp.einsum('bhncd,bhnsd->bhncs', kb, k) * decay, 0.0)
    rhs = jnp.concatenate([v * beta[..., None], kb * jnp.exp(gc)[..., None]], axis=-1)
    sol = lax.linalg.triangular_solve(a_mat + jnp.eye(c, dtype=f32), rhs,
                                      left_side=True, lower=True, unit_diagonal=True)
    u, w = sol[..., :dv], sol[..., dv:]
    qk = jnp.where(incl, jnp.einsum('bhncd,bhnsd->bhncs', q, k) * decay, 0.0)

    def step(state, xs):
        qc, kc, uc, wc, gcc, qkc = xs
        v_new = uc - jnp.einsum('bhck,bhkv->bhcv', wc, state)
        o = (jnp.einsum('bhck,bhkv->bhcv', qc * jnp.exp(gcc)[..., None], state)
             + jnp.einsum('bhcs,bhsv->bhcv', qkc, v_new))
        g_last = gcc[..., -1]
        state = (state * jnp.exp(g_last)[..., None, None]
                 + jnp.einsum('bhck,bhcv->bhkv', kc * jnp.exp(g_last[..., None] - gcc)[..., None], v_new))
        return state, o

    xs = tuple(jnp.moveaxis(t, 2, 0) for t in (q, k, u, w, gc, qk))
    s_fin, o = lax.scan(step, s0.astype(f32), xs)
    o = jnp.transpose(o, (1, 0, 3, 2, 4)).reshape(bsz, n * c, h, dv)[:, :s]
    return o, s_fin.astype(s0.dtype)


def gdn_branch(qkv, z, a, b, conv_buf, s0, p):
    bsz, s = qkv.shape[:2]
    f32 = jnp.float32
    qkv_c, new_buf = causal_conv(qkv, conv_buf, p['conv_w'])
    q = qkv_c[..., :GDN_QK_W].reshape(bsz, s, GDN_HEADS, GDN_DK)
    k = qkv_c[..., GDN_QK_W:2 * GDN_QK_W].reshape(bsz, s, GDN_HEADS, GDN_DK)
    v = qkv_c[..., 2 * GDN_QK_W:].reshape(bsz, s, GDN_HEADS, GDN_DV)
    g = -jnp.exp(p['gdn_a_log'].astype(f32)) * jax.nn.softplus(a.astype(f32) + p['gdn_dt_bias'].astype(f32))
    beta = jax.nn.sigmoid(b.astype(f32))
    o, s_new = gated_delta_rule(l2_normalize(q) * GDN_DK ** -0.5, l2_normalize(k), v, g, beta, s0)
    o = rms_norm(o.astype(qkv.dtype), p['gdn_norm_g']) * jax.nn.silu(z.reshape(bsz, s, GDN_HEADS, GDN_DV))
    return o.reshape(bsz, s, GDN_V_W), new_buf, s_new


def mla_project(q_a, kv_a, pos, p):
    bsz, s = q_a.shape[:2]
    cq = rms_norm(q_a, p['q_a_norm_g'])
    q = _matmul(cq.reshape(bsz * s, Q_LORA), p['w_qb'].reshape(Q_LORA, -1)).reshape(
        bsz, s, MLA_HEADS, NOPE + ROPE)
    q_nope = rms_norm(q[..., :NOPE], p['q_nope_norm_g'])
    q_rope = rotary(rms_norm(q[..., NOPE:], p['q_rope_norm_g']), pos)
    ckv = rms_norm(kv_a[..., :KV_LORA], p['kv_a_norm_g'])
    k_rope = rotary(rms_norm(kv_a[..., KV_LORA:], p['k_rope_norm_g'])[:, :, None, :], pos)[:, :, 0]
    return q_nope, q_rope, ckv, k_rope


def mla_expand(ckv, p):
    kv = jnp.einsum('bsc,chd->bshd', ckv, p['w_kvb'])
    return rms_norm(kv[..., :NOPE], p['k_nope_norm_g']), kv[..., NOPE:]


def mla_scores(q_nope, q_rope, k_nope, k_rope):
    s = jnp.einsum('bthd,bkhd->bhtk', q_nope, k_nope) + jnp.einsum('bthr,bkr->bhtk', q_rope, k_rope)
    return s.astype(jnp.float32) * ATTN_SCALE


def mla_prompt_attention(q_nope, q_rope, ckv, k_rope, p):
    bsz, s = q_nope.shape[:2]
    k_nope, v = mla_expand(ckv, p)
    qb = min(Q_BLOCK, s)
    nb = s // qb
    qn = q_nope.reshape(bsz, nb, qb, MLA_HEADS, NOPE).swapaxes(0, 1)
    qr = q_rope.reshape(bsz, nb, qb, MLA_HEADS, ROPE).swapaxes(0, 1)
    kpos = jnp.arange(s, dtype=jnp.int32)

    def block(args):
        i, qn_b, qr_b = args
        sc = mla_scores(qn_b, qr_b, k_nope, k_rope)
        qpos = i * qb + jnp.arange(qb, dtype=jnp.int32)
        sc = jnp.where(kpos[None, :] <= qpos[:, None], sc, -jnp.inf)
        pr = jax.nn.softmax(sc, axis=-1)
        return jnp.einsum('bhqk,bkhd->bqhd', pr.astype(v.dtype), v)

    o = lax.map(block, (jnp.arange(nb, dtype=jnp.int32), qn, qr))
    return o.swapaxes(0, 1).reshape(bsz, s, MLA_V_W)


def mla_sample_attention(q_nope, q_rope, ckv, k_rope, p, cache_latent, cache_krope, page_table):
    f32 = jnp.float32
    bsz, t = q_nope.shape[:2]
    n_pages = page_table.shape[1]
    pb = math.gcd(n_pages, KV_BLOCK_PAGES)
    blocks = page_table.reshape(bsz, n_pages // pb, pb).swapaxes(0, 1)

    def merge(carry, sc, v):
        m, l, acc = carry
        m_new = jnp.maximum(m, jnp.max(sc, axis=-1))
        e = jnp.exp(sc - m_new[..., None])
        corr = jnp.exp(m - m_new)
        acc = acc * corr[..., None] + jnp.einsum('bhtk,bkhd->bhtd', e, v.astype(f32))
        return (m_new, l * corr + jnp.sum(e, axis=-1), acc)

    def step(carry, pages):
        lat = cache_latent[pages].reshape(bsz, pb * PAGE_SIZE, KV_LORA)
        kr = cache_krope[pages].reshape(bsz, pb * PAGE_SIZE, ROPE)
        k_nope, v = mla_expand(lat, p)
        return merge(carry, mla_scores(q_nope, q_rope, k_nope, kr), v), None

    init = (jnp.full((bsz, MLA_HEADS, t), -jnp.inf, f32),
            jnp.zeros((bsz, MLA_HEADS, t), f32),
            jnp.zeros((bsz, MLA_HEADS, t, V_DIM), f32))
    carry, _ = lax.scan(step, init, blocks)
    k_nope_n, v_n = mla_expand(ckv, p)
    sc = mla_scores(q_nope, q_rope, k_nope_n, k_rope)
    sc = jnp.where(jnp.tril(jnp.ones((t, t), dtype=bool)), sc, -jnp.inf)
    m, l, acc = merge(carry, sc, v_n)
    o = acc / l[..., None]
    return jnp.transpose(o, (0, 2, 1, 3)).reshape(bsz, t, MLA_V_W).astype(q_nope.dtype)


_NT = (((1,), (1,)), ((), ()))

_CAND_BLOCKS = ((0, 0, 8), (0, 8, 8), (1, 0, 8), (2, 0, 5), (3, 0, 4), (4, 0, 3), (5, 0, 2), (6, 0, 2), (7, 0, 2))
_SUBLANES = 8


def _extract_top16(s, vals_ref, idx_ref):
    n = s.shape[0]
    row = lax.broadcasted_iota(jnp.int32, s.shape, 0).astype(jnp.float32)
    for k in range(PEER_TOPK):
        m = jnp.max(s, axis=0, keepdims=True)
        i = jnp.min(jnp.where(s == m, row, float(n)), axis=0, keepdims=True)
        vals_ref[k:k + 1, :] = m
        idx_ref[k:k + 1, :] = i
        s = jnp.where(row == i, -jnp.inf, s)


def _peer_retrieve_body(x_ref, wqT_ref, keys_ref, e_ref, g_ref, s1_ref, i1_ref, s2_ref, i2_ref, ts_ref):
    tm = x_ref.shape[0]
    qT = lax.dot_general(wqT_ref[...], x_ref[...], _NT, preferred_element_type=jnp.float32)
    for c, (s_ref, i_ref) in enumerate(((s1_ref, i1_ref), (s2_ref, i2_ref))):
        sT = jnp.dot(keys_ref[0, c], qT[c * N_KEYS:(c + 1) * N_KEYS].astype(jnp.bfloat16),
                     preferred_element_type=jnp.float32)
        _extract_top16(sT, s_ref, i_ref)
    sub = lax.broadcasted_iota(jnp.int32, (_SUBLANES, tm), 0).astype(jnp.float32)
    cs, js, es = [], [], []
    for a, b0, nv in _CAND_BLOCKS:
        c = s1_ref[a:a + 1, :] + s2_ref[b0:b0 + _SUBLANES, :]
        cs.append(jnp.where(sub < float(nv), c, -jnp.inf))
        js.append(sub + float(a * PEER_TOPK + b0))
        es.append(i1_ref[a:a + 1, :] * float(N_KEYS) + i2_ref[b0:b0 + _SUBLANES, :])
    cs.append(s1_ref[_SUBLANES:, :] + s2_ref[0:1, :])
    js.append((sub + float(_SUBLANES)) * float(PEER_TOPK))
    es.append(i1_ref[_SUBLANES:, :] * float(N_KEYS) + i2_ref[0:1, :])
    big = float(PEER_TOPK * PEER_TOPK)
    for k in range(PEER_TOPK):
        m = functools.reduce(jnp.maximum, cs)
        m = jnp.max(m, axis=0, keepdims=True)
        jm = functools.reduce(jnp.minimum, [jnp.where(c == m, j, big) for c, j in zip(cs, js)])
        jm = jnp.min(jm, axis=0, keepdims=True)
        hit = [j == jm for j in js]
        em = functools.reduce(jnp.maximum, [jnp.where(h, e, -1.0) for h, e in zip(hit, es)])
        ts_ref[k:k + 1, :] = m
        e_ref[0, k:k + 1, :] = jnp.max(em, axis=0, keepdims=True)
        cs = [jnp.where(h, -jnp.inf, c) for h, c in zip(hit, cs)]
    ts = ts_ref[...]
    ex = jnp.exp(ts - ts[0:1, :])
    g_ref[0] = ex / jnp.sum(ex, axis=0, keepdims=True)


def peer_retrieve(h2, wqT, keys, *, tm):
    t = h2.shape[0]
    out = jax.ShapeDtypeStruct((PEER_HEADS, PEER_TOPK, t), jnp.float32)
    scr = pltpu.VMEM((PEER_TOPK, tm), jnp.float32)
    return pl.pallas_call(
        _peer_retrieve_body,
        grid=(t // tm, PEER_HEADS),
        in_specs=[pl.BlockSpec((tm, D_MODEL), lambda i, h: (i, 0)),
                  pl.BlockSpec((PEER_DKEY, D_MODEL), lambda i, h: (h, 0)),
                  pl.BlockSpec((1, 2, N_KEYS, PEER_DKEY // 2), lambda i, h: (h, 0, 0, 0))],
        out_specs=[pl.BlockSpec((1, PEER_TOPK, tm), lambda i, h: (h, 0, i)),
                   pl.BlockSpec((1, PEER_TOPK, tm), lambda i, h: (h, 0, i))],
        out_shape=[out, out],
        scratch_shapes=[scr, scr, scr, scr, scr],
        compiler_params=pltpu.CompilerParams(dimension_semantics=("parallel", "arbitrary")),
        name="peer_retrieve",
    )(h2, wqT, keys)


_W_PITCH = 136


def _peer_wbuild_body(e_ref, g_ref, w_ref, i1_ref, i2_ref, gt_ref, s_ref):
    tw = w_ref.shape[0]
    hk = PEER_HEADS * PEER_TOPK
    eT = e_ref[...].reshape(hk, tw).T
    i1 = jnp.floor(eT * (1.0 / N_KEYS))
    i1_ref[...] = i1
    i2_ref[...] = eT - i1 * float(N_KEYS)
    gt_ref[...] = g_ref[...].reshape(hk, tw).T
    row = lax.broadcasted_iota(jnp.int32, (N_KEYS, hk), 0).astype(jnp.float32)

    def body(t, carry):
        a1 = jnp.where(row == i1_ref[pl.ds(t, 1), :], 1.0, 0.0).astype(jnp.bfloat16)
        a2 = jnp.where(row == i2_ref[pl.ds(t, 1), :], gt_ref[pl.ds(t, 1), :], 0.0).astype(jnp.bfloat16)
        wt = lax.dot_general(a1, a2, _NT, preferred_element_type=jnp.float32)
        s_ref[pl.ds(t, N_KEYS, stride=_W_PITCH), :] = wt
        return carry

    lax.fori_loop(0, tw, body, 0, unroll=_SUBLANES)
    for i in range(N_KEYS):
        w_ref[:, i * N_KEYS:(i + 1) * N_KEYS] = s_ref[i * _W_PITCH:i * _W_PITCH + tw, :].astype(jnp.bfloat16)


def peer_wbuild(e, g, *, tw=128):
    t = e.shape[-1]
    hk = PEER_HEADS * PEER_TOPK
    tok = pltpu.VMEM((tw, hk), jnp.float32)
    return pl.pallas_call(
        _peer_wbuild_body,
        grid=(t // tw,),
        in_specs=[pl.BlockSpec((PEER_HEADS, PEER_TOPK, tw), lambda i: (0, 0, i)),
                  pl.BlockSpec((PEER_HEADS, PEER_TOPK, tw), lambda i: (0, 0, i))],
        out_specs=pl.BlockSpec((tw, N_EXPERTS), lambda i: (i, 0)),
        out_shape=jax.ShapeDtypeStruct((t, N_EXPERTS), jnp.bfloat16),
        scratch_shapes=[tok, tok, tok, pltpu.VMEM((N_KEYS * _W_PITCH, N_KEYS), jnp.float32)],
        compiler_params=pltpu.CompilerParams(dimension_semantics=("parallel",),
                                             vmem_limit_bytes=40 * 1024 * 1024),
        name="peer_wbuild",
    )(e, g)


def _peer_dense_body(x_ref, u_ref, v_ref, w_ref, res_ref, g2_ref, o_ref, acc_ref):
    j = pl.program_id(1)

    @pl.when(j == 0)
    def _():
        acc_ref[...] = jnp.zeros_like(acc_ref)

    h = lax.dot_general(x_ref[...], u_ref[...], _NT, preferred_element_type=jnp.float32)
    act = 0.5 * h * (1.0 + lax.erf(h * (2.0 ** -0.5)))
    p = (act * w_ref[...].astype(jnp.float32)).astype(jnp.bfloat16)
    acc_ref[...] += jnp.dot(p, v_ref[...], preferred_element_type=jnp.float32)

    @pl.when(j == pl.num_programs(1) - 1)
    def _():
        o_ref[...] = res_ref[...] + g2_ref[0] * acc_ref[...]


def peer_dense(h2, u, v, w, res, g2, *, tm, te, tokens_per_gate_row):
    t = h2.shape[0]
    g_rows = g2.shape[1]
    tiles_per_gate = tokens_per_gate_row // tm
    return pl.pallas_call(
        _peer_dense_body,
        grid=(t // tm, N_EXPERTS // te),
        in_specs=[pl.BlockSpec((tm, D_MODEL), lambda i, j: (i, 0)),
                  pl.BlockSpec((te, D_MODEL), lambda i, j: (j, 0)),
                  pl.BlockSpec((te, D_MODEL), lambda i, j: (j, 0)),
                  pl.BlockSpec((tm, te), lambda i, j: (i, j)),
                  pl.BlockSpec((tm, D_MODEL), lambda i, j: (i, 0)),
                  pl.BlockSpec((1, g_rows, D_MODEL), lambda i, j: (i // tiles_per_gate, 0, 0))],
        out_specs=pl.BlockSpec((tm, D_MODEL), lambda i, j: (i, 0)),
        out_shape=jax.ShapeDtypeStruct((t, D_MODEL), jnp.float32),
        scratch_shapes=[pltpu.VMEM((tm, D_MODEL), jnp.float32)],
        compiler_params=pltpu.CompilerParams(dimension_semantics=("parallel", "arbitrary"),
                                             vmem_limit_bytes=48 * 1024 * 1024),
        name="peer_dense",
    )(h2, u, v, w, res, g2)


def peer_residual(x, h2, g2, p):
    bsz, s, d = x.shape
    n_tok = bsz * s
    wqT = p['peer_wq'].T.astype(jnp.bfloat16)
    keys = p['peer_keys'].astype(jnp.bfloat16)
    u = p['peer_u'].astype(jnp.bfloat16)
    v = p['peer_v'].astype(jnp.bfloat16)
    h2 = h2.reshape(n_tok, d).astype(jnp.bfloat16)
    res = x.reshape(n_tok, d)
    if s % 1024 == 0:
        tm_r, tm_d, te, per_gate = 256, 1024, 512, s
    else:
        pad = _round_up(n_tok, LANES) - n_tok
        h2 = jnp.pad(h2, ((0, pad), (0, 0)))
        res = jnp.pad(res, ((0, pad), (0, 0)))
        g2 = jnp.pad(jnp.broadcast_to(g2, (bsz, s, d)).reshape(1, n_tok, d), ((0, 0), (0, pad), (0, 0)))
        tm_r = tm_d = per_gate = n_tok + pad
        te = 512
    e, g = peer_retrieve(h2, wqT, keys, tm=tm_r)
    w = peer_wbuild(e, g)
    out = peer_dense(h2, u, v, w, res, g2, tm=tm_d, te=te, tokens_per_gate_row=per_gate)
    return out[:n_tok].reshape(bsz, s, d)


def trunk_layer(x, c, pos, conv_buf, gdn_state, attend, p):
    bsz, s, d = x.shape
    mod = jax.nn.silu(c) @ p['w_ada'] + p['b_ada']
    sh1, sc1, g1, sh2, sc2, g2 = jnp.split(mod[:, None, :], 6, axis=-1)
    h = rms_norm(x, p['norm1_g']) * (1 + sc1) + sh1
    proj = _matmul(h.reshape(bsz * s, d), p['w_in']).reshape(bsz, s, -1)
    qkv, z, a, b, q_a, kv_a, gates = jnp.split(proj, IN_OFFSETS, axis=-1)
    o_a, new_conv, new_s = gdn_branch(qkv, z, a, b, conv_buf, gdn_state, p)
    q_nope, q_rope, ckv, k_rope = mla_project(q_a, kv_a, pos, p)
    o_b = attend(q_nope, q_rope, ckv, k_rope, p)
    gate_a, gate_b = jnp.split(jax.nn.sigmoid(gates), 2, axis=-1)
    ma = _matmul(o_a.reshape(bsz * s, -1), p['w_branch_a']).reshape(bsz, s, d)
    mb = _matmul(o_b.reshape(bsz * s, -1), p['w_branch_b']).reshape(bsz, s, d)
    merged = gate_a * ma + gate_b * mb
    x = x + g1 * _matmul(merged.reshape(bsz * s, d), p['w_out']).reshape(bsz, s, d)
    h2 = rms_norm(x, p['norm2_g']) * (1 + sc2) + sh2
    x = peer_residual(x, h2, g2, p)
    return x, ckv, k_rope, new_s, new_conv


def kernel(x_prompt, x_sample, c_prompt, c_sample, cache_latent, cache_krope, state_gdn, state_conv,
           page_table, w_ada, b_ada, norm1_g, norm2_g, w_in, conv_w, gdn_a_log, gdn_dt_bias, gdn_norm_g,
           q_a_norm_g, w_qb, kv_a_norm_g, w_kvb, q_nope_norm_g, q_rope_norm_g, k_nope_norm_g, k_rope_norm_g,
           w_branch_a, w_branch_b, w_out, peer_wq, peer_keys, peer_u, peer_v):
    bsz_p, seq_p = x_prompt.shape[:2]
    pos_p = jnp.arange(seq_p, dtype=jnp.int32)
    past = page_table.shape[1] * PAGE_SIZE
    pos_s = past + jnp.arange(x_sample.shape[1], dtype=jnp.int32)
    l = 0
    p = {
        'w_ada': w_ada[l], 'b_ada': b_ada[l], 'norm1_g': norm1_g[l], 'norm2_g': norm2_g[l],
        'w_in': w_in[l], 'conv_w': conv_w[l], 'gdn_a_log': gdn_a_log[l], 'gdn_dt_bias': gdn_dt_bias[l],
        'gdn_norm_g': gdn_norm_g[l], 'q_a_norm_g': q_a_norm_g[l], 'w_qb': w_qb[l],
        'kv_a_norm_g': kv_a_norm_g[l], 'w_kvb': w_kvb[l], 'q_nope_norm_g': q_nope_norm_g[l],
        'q_rope_norm_g': q_rope_norm_g[l], 'k_nope_norm_g': k_nope_norm_g[l],
        'k_rope_norm_g': k_rope_norm_g[l], 'w_branch_a': w_branch_a[l], 'w_branch_b': w_branch_b[l],
        'w_out': w_out[l], 'peer_wq': peer_wq[l], 'peer_keys': peer_keys[l], 'peer_u': peer_u[l],
        'peer_v': peer_v[l],
    }
    conv0 = jnp.zeros((bsz_p, CONV_K - 1, CONV_CH), x_prompt.dtype)
    gdn0 = jnp.zeros((bsz_p, GDN_HEADS, GDN_DK, GDN_DV), x_prompt.dtype)
    xp, lat_p, kr_p, g_p, cv_p = trunk_layer(x_prompt, c_prompt, pos_p, conv0, gdn0, mla_prompt_attention, p)
    attend = functools.partial(mla_sample_attention, cache_latent=cache_latent[l],
                               cache_krope=cache_krope[l], page_table=page_table)
    xs, lat_s, kr_s, g_s, cv_s = trunk_layer(x_sample, c_sample, pos_s, state_conv[l], state_gdn[l], attend, p)
    st = lambda t: t[None]
    return (xp, xs, st(lat_p), st(kr_p), st(g_p), st(cv_p), st(lat_s), st(kr_s), st(g_s), st(cv_s))
```

```python
import functools
import math

import jax
import jax.numpy as jnp
from jax import lax
from jax.experimental import pallas as pl
from jax.experimental.pallas import tpu as pltpu

D_MODEL = 1024
PAGE_SIZE = 128
GDN_HEADS = 8
GDN_DK = 128
GDN_DV = 128
GDN_QK_W = GDN_HEADS * GDN_DK
GDN_V_W = GDN_HEADS * GDN_DV
CONV_K = 4
CONV_CH = 2 * GDN_QK_W + GDN_V_W
GDN_CHUNK = 64
MLA_HEADS = 8
Q_LORA = 384
KV_LORA = 256
NOPE = 128
ROPE = 64
V_DIM = 128
MLA_V_W = MLA_HEADS * V_DIM
ROPE_THETA = 10000.0
ATTN_SCALE = (NOPE + ROPE) ** -0.5
Q_BLOCK = 128
KV_BLOCK_PAGES = 4
PEER_HEADS = 8
N_KEYS = 128
N_EXPERTS = N_KEYS * N_KEYS
PEER_DKEY = 256
PEER_TOPK = 16
PEER_BLOCK = 128
EPS = 1e-6
IN_OFFSETS = (
    CONV_CH,
    CONV_CH + GDN_V_W,
    CONV_CH + GDN_V_W + GDN_HEADS,
    CONV_CH + GDN_V_W + 2 * GDN_HEADS,
    CONV_CH + GDN_V_W + 2 * GDN_HEADS + Q_LORA,
    CONV_CH + GDN_V_W + 2 * GDN_HEADS + Q_LORA + KV_LORA + ROPE,
)

LANES = 128
_SUBLANES = 8
_NT = (((1,), (1,)), ((), ()))
_TN = (((0,), (0,)), ((), ()))


def _round_up(x, m):
    return (x + m - 1) // m * m


def _matmul_body(x_ref, w_ref, o_ref):
    o_ref[...] = jnp.dot(x_ref[...].astype(jnp.bfloat16), w_ref[...],
                         preferred_element_type=jnp.float32)


def _matmul(x, w, *, tm=512, tn=512):
    m, k = x.shape
    n = w.shape[1]
    n_pad = _round_up(n, tn)
    m_pad = _round_up(m, 8)
    tm = min(tm, m_pad)
    m_pad = _round_up(m_pad, tm)
    wb = w.astype(jnp.bfloat16)
    if n_pad != n:
        wb = jnp.pad(wb, ((0, 0), (0, n_pad - n)))
    if m_pad != m:
        x = jnp.pad(x, ((0, m_pad - m), (0, 0)))
    out = pl.pallas_call(
        _matmul_body,
        grid=(m_pad // tm, n_pad // tn),
        in_specs=[pl.BlockSpec((tm, k), lambda i, j: (i, 0)),
                  pl.BlockSpec((k, tn), lambda i, j: (0, j))],
        out_specs=pl.BlockSpec((tm, tn), lambda i, j: (i, j)),
        out_shape=jax.ShapeDtypeStruct((m_pad, n_pad), jnp.float32),
        compiler_params=pltpu.CompilerParams(
            dimension_semantics=("parallel", "parallel")),
        name="dense_matmul",
    )(x, wb)
    return out[:m, :n]


def rms_norm(x, g):
    xf = x.astype(jnp.float32)
    y = xf * lax.rsqrt(jnp.mean(xf * xf, axis=-1, keepdims=True) + EPS)
    return (y * g.astype(jnp.float32)).astype(x.dtype)


def l2_normalize(x):
    xf = x.astype(jnp.float32)
    return xf * lax.rsqrt(jnp.sum(xf * xf, axis=-1, keepdims=True) + EPS)


def rotary(x, pos):
    half = ROPE // 2
    inv_freq = ROPE_THETA ** (-jnp.arange(half, dtype=jnp.float32) / half)
    ang = pos.astype(jnp.float32)[:, None] * inv_freq[None, :]
    cos = jnp.cos(ang)[None, :, None, :]
    sin = jnp.sin(ang)[None, :, None, :]
    xf = x.astype(jnp.float32)
    x1, x2 = xf[..., :half], xf[..., half:]
    return jnp.concatenate([x1 * cos - x2 * sin, x2 * cos + x1 * sin], axis=-1).astype(x.dtype)


def causal_conv(x, buf, w):
    s = x.shape[1]
    xp = jnp.concatenate([buf.astype(x.dtype), x], axis=1)
    y = xp[:, 0:s] * w[0]
    for j in range(1, CONV_K):
        y = y + xp[:, j:j + s] * w[j]
    return jax.nn.silu(y), xp[:, s:]


def gated_delta_rule(q, k, v, g, beta, s0):
    f32 = jnp.float32
    bsz, s, h, dk = q.shape
    dv = v.shape[-1]
    c = GDN_CHUNK
    pad = (-s) % c
    n = (s + pad) // c

    def chunks(t):
        t = jnp.pad(t.astype(f32), [(0, 0), (0, pad)] + [(0, 0)] * (t.ndim - 2))
        t = t.reshape((bsz, n, c) + t.shape[2:])
        return jnp.moveaxis(t, 3, 1)

    q, k, v, g, beta = map(chunks, (q, k, v, g, beta))
    gc = jnp.cumsum(g, axis=-1)
    incl = jnp.tril(jnp.ones((c, c), dtype=bool))
    strict = jnp.tril(jnp.ones((c, c), dtype=bool), -1)
    diff = gc[..., :, None] - gc[..., None, :]
    decay = jnp.where(incl, jnp.exp(jnp.where(incl, diff, 0.0)), 0.0)
    kb = k * beta[..., None]
    a_mat = jnp.where(strict, jnp.einsum('bhncd,bhnsd->bhncs', kb, k) * decay, 0.0)
    rhs = jnp.concatenate([v * beta[..., None], kb * jnp.exp(gc)[..., None]], axis=-1)
    sol = lax.linalg.triangular_solve(a_mat + jnp.eye(c, dtype=f32), rhs,
                                      left_side=True, lower=True, unit_diagonal=True)
    u, w = sol[..., :dv], sol[..., dv:]
    qk = jnp.where(incl, jnp.einsum('bhncd,bhnsd->bhncs', q, k) * decay, 0.0)

    def step(state, xs):
        qc, kc, uc, wc, gcc, qkc = xs
        v_new = uc - jnp.einsum('bhck,bhkv->bhcv', wc, state)
        o = (jnp.einsum('bhck,bhkv->bhcv', qc * jnp.exp(gcc)[..., None], state)
             + jnp.einsum('bhcs,bhsv->bhcv', qkc, v_new))
        g_last = gcc[..., -1]
        state = (state * jnp.exp(g_last)[..., None, None]
                 + jnp.einsum('bhck,bhcv->bhkv', kc * jnp.exp(g_last[..., None] - gcc)[..., None], v_new))
        return state, o

    xs = tuple(jnp.moveaxis(t, 2, 0) for t in (q, k, u, w, gc, qk))
    s_fin, o = lax.scan(step, s0.astype(f32), xs)
    o = jnp.transpose(o, (1, 0, 3, 2, 4)).reshape(bsz, n * c, h, dv)[:, :s]
    return o, s_fin.astype(s0.dtype)


def gdn_branch(qkv, z, a, b, conv_buf, s0, p):
    bsz, s = qkv.shape[:2]
    f32 = jnp.float32
    qkv_c, new_buf = causal_conv(qkv, conv_buf, p['conv_w'])
    q = qkv_c[..., :GDN_QK_W].reshape(bsz, s, GDN_HEADS, GDN_DK)
    k = qkv_c[..., GDN_QK_W:2 * GDN_QK_W].reshape(bsz, s, GDN_HEADS, GDN_DK)
    v = qkv_c[..., 2 * GDN_QK_W:].reshape(bsz, s, GDN_HEADS, GDN_DV)
    g = -jnp.exp(p['gdn_a_log'].astype(f32)) * jax.nn.softplus(a.astype(f32) + p['gdn_dt_bias'].astype(f32))
    beta = jax.nn.sigmoid(b.astype(f32))
    o, s_new = gated_delta_rule(l2_normalize(q) * GDN_DK ** -0.5, l2_normalize(k), v, g, beta, s0)
    o = rms_norm(o.astype(qkv.dtype), p['gdn_norm_g']) * jax.nn.silu(z.reshape(bsz, s, GDN_HEADS, GDN_DV))
    return o.reshape(bsz, s, GDN_V_W), new_buf, s_new


_DIAG_BLOCK = 16
_GROUP = 8


def _mm(a, b):
    return jnp.dot(a.astype(jnp.bfloat16), b.astype(jnp.bfloat16), preferred_element_type=jnp.float32)


def _mm_nt(a, b):
    return lax.dot_general(a.astype(jnp.bfloat16), b.astype(jnp.bfloat16), _NT, preferred_element_type=jnp.float32)


def _conv_silu(x, w_ref):
    row = lax.broadcasted_iota(jnp.int32, x.shape, 0)
    y = x * w_ref[0, CONV_K - 1:CONV_K, :]
    for j in range(1, CONV_K):
        xs = jnp.where(row >= j, pltpu.roll(x, j, axis=0), 0.0)
        y = y + xs * w_ref[0, CONV_K - 1 - j:CONV_K - j, :]
    return y * jax.nn.sigmoid(y)


def _l2n(x):
    return x * lax.rsqrt(jnp.sum(x * x, axis=-1, keepdims=True) + EPS)


def _unit_lower_inverse(a_list):
    c = a_list[0].shape[0]
    ri = lax.broadcasted_iota(jnp.int32, (c, c), 0)
    ci = lax.broadcasted_iota(jnp.int32, (c, c), 1)
    eye = jnp.where(ri == ci, 1.0, 0.0)
    same_block = (ri // _DIAG_BLOCK) == (ci // _DIAG_BLOCK)
    pw = [jnp.where(same_block, -a, 0.0) for a in a_list]
    t = [eye + p for p in pw]
    for _ in range(3):
        pw = [_mm(p, p) for p in pw]
        t = [ti + _mm(ti, p) for ti, p in zip(t, pw)]
    m = [_mm(ti, jnp.where(same_block, 0.0, a)) for ti, a in zip(t, a_list)]
    m2 = [_mm(mi, mi) for mi in m]
    m3 = [_mm(mi, m2i) for mi, m2i in zip(m, m2)]
    return [_mm(eye - mi + m2i - m3i, ti) for mi, m2i, m3i, ti in zip(m, m2, m3, t)]


def _gdn_prompt_body(q_ref, k_ref, v_ref, z_ref, wq_ref, wk_ref, wv_ref, gcr_ref, gcc_ref, bc_ref, gn_ref,
                     o_ref, st_ref, qn_ref, kn_ref, vv_ref, qe_ref, o0_ref, m_ref, n_ref, s_ref):
    c = GDN_CHUNK
    s = q_ref.shape[1]
    n_chunks = s // c
    qn_ref[...] = _l2n(_conv_silu(q_ref[0], wq_ref)) * (GDN_DK ** -0.5)
    kn_ref[...] = _l2n(_conv_silu(k_ref[0], wk_ref))
    vv_ref[...] = _conv_silu(v_ref[0], wv_ref)
    ri = lax.broadcasted_iota(jnp.int32, (c, c), 0)
    ci = lax.broadcasted_iota(jnp.int32, (c, c), 1)

    def group_terms(ins):
        f32, bf16 = jnp.float32, jnp.bfloat16
        qs, ks, vs, betas, gccs, gcrs = zip(*ins)
        decay = [jnp.where(ri >= ci, jnp.exp(jnp.where(ri >= ci, gcc - gcr, 0.0)), 0.0)
                 for gcc, gcr in zip(gccs, gcrs)]
        kb = [k * b for k, b in zip(ks, betas)]
        kk = [_mm_nt(kbi, k) for kbi, k in zip(kb, ks)]
        qk = [_mm_nt(q, k) for q, k in zip(qs, ks)]
        t = _unit_lower_inverse([jnp.where(ri > ci, x * d, 0.0) for x, d in zip(kk, decay)])
        sol = [_mm(ti, jnp.concatenate([v * b, kbi * jnp.exp(gcc)], axis=-1))
               for ti, v, b, kbi, gcc in zip(t, vs, betas, kb, gccs)]
        u = [x[:, :GDN_DV].astype(bf16) for x in sol]
        w = [x[:, GDN_DV:].astype(bf16) for x in sol]
        qk = [jnp.where(ri >= ci, x * d, 0.0).astype(bf16) for x, d in zip(qk, decay)]
        kd = [(k * jnp.exp(gcc[c - 1:c, :] - gcc)).astype(bf16) for k, gcc in zip(ks, gccs)]
        qw = [jnp.dot(a, b, preferred_element_type=f32) for a, b in zip(qk, w)]
        o0 = [jnp.dot(a, b, preferred_element_type=f32) for a, b in zip(qk, u)]
        m = [lax.dot_general(a, b, _TN, preferred_element_type=f32) for a, b in zip(kd, w)]
        nn = [lax.dot_general(a, b, _TN, preferred_element_type=f32) for a, b in zip(kd, u)]
        qe = [q * jnp.exp(gcc) - x for q, gcc, x in zip(qs, gccs, qw)]
        return list(zip(qe, o0, m, nn))

    def solve_group(gi, carry):
        ins = []
        for j in range(_GROUP):
            n = gi * _GROUP + j
            rows = pl.ds(pl.multiple_of(n * c, c), c)
            ins.append((qn_ref[rows, :], kn_ref[rows, :], vv_ref[rows, :], bc_ref[0, 0, n], gcc_ref[0, 0, n],
                        gcr_ref[0, 0, pl.ds(n, 1), :]))
        outs = group_terms(ins)
        for j, (qe, o0, m, nn) in enumerate(outs):
            n = gi * _GROUP + j
            rows = pl.ds(pl.multiple_of(n * c, c), c)
            qe_ref[rows, :] = qe
            o0_ref[rows, :] = o0
            m_ref[n] = m
            n_ref[n] = nn
        return carry

    lax.fori_loop(0, n_chunks // _GROUP, solve_group, 0)

    def scan_chunk(n, state):
        s_ref[n] = state
        g_last = gcc_ref[0, 0, n][c - 1:c, :]
        return state * jnp.exp(g_last) - _mm(m_ref[n], state) + n_ref[n]

    st_ref[0, 0] = lax.fori_loop(0, n_chunks, scan_chunk, jnp.zeros((GDN_DK, GDN_DV), jnp.float32))

    def out_group(gi, carry):
        ins = []
        for j in range(_GROUP):
            n = gi * _GROUP + j
            rows = pl.ds(pl.multiple_of(n * c, c), c)
            ins.append((qe_ref[rows, :], s_ref[n], o0_ref[rows, :], z_ref[0, rows, :]))
        os_ = [_mm(qe, st) + o0 for qe, st, o0, _ in ins]
        outs = []
        for o, (_, _, _, zz) in zip(os_, ins):
            on = o * lax.rsqrt(jnp.mean(o * o, axis=-1, keepdims=True) + EPS) * gn_ref[...]
            outs.append(on * (zz * jax.nn.sigmoid(zz)))
        for j, val in enumerate(outs):
            n = gi * _GROUP + j
            o_ref[0, pl.ds(pl.multiple_of(n * c, c), c), :] = val
        return carry

    lax.fori_loop(0, n_chunks // _GROUP, out_group, 0)


def gdn_prompt(proj, conv_w, gc, beta, gdn_norm_g):
    bsz, s, _ = proj.shape
    h = GDN_HEADS
    c = GDN_CHUNK
    assert s % (c * _GROUP) == 0
    n = s // c
    gct = jnp.transpose(gc, (0, 2, 1)).reshape(bsz, h, n, c)
    bt = jnp.transpose(beta, (0, 2, 1)).reshape(bsz, h, n, c)
    cw = conv_w.reshape(1, CONV_K, CONV_CH)
    col = lambda off: pl.BlockSpec((1, s, GDN_DK), lambda b, hh: (b, 0, off + hh))
    wcol = lambda off: pl.BlockSpec((1, CONV_K, GDN_DK), lambda b, hh: (0, 0, off + hh))
    per_chunk_col = pl.BlockSpec((1, 1, n, c, 1), lambda b, hh: (b, hh, 0, 0, 0))
    seq = pltpu.VMEM((s, GDN_DK), jnp.float32)
    return pl.pallas_call(
        _gdn_prompt_body,
        grid=(bsz, h),
        in_specs=[col(0), col(h), col(2 * h), col(3 * h),
                  wcol(0), wcol(h), wcol(2 * h),
                  pl.BlockSpec((1, 1, n, c), lambda b, hh: (b, hh, 0, 0)),
                  per_chunk_col, per_chunk_col,
                  pl.BlockSpec((1, GDN_DV), lambda b, hh: (0, 0))],
        out_specs=[pl.BlockSpec((1, s, GDN_DV), lambda b, hh: (b, 0, hh)),
                   pl.BlockSpec((1, 1, GDN_DK, GDN_DV), lambda b, hh: (b, hh, 0, 0))],
        out_shape=[jax.ShapeDtypeStruct((bsz, s, h * GDN_DV), jnp.float32),
                   jax.ShapeDtypeStruct((bsz, h, GDN_DK, GDN_DV), jnp.float32)],
        scratch_shapes=[seq, seq, seq, seq, seq] + [pltpu.VMEM((n, GDN_DK, GDN_DV), jnp.float32)] * 3,
        compiler_params=pltpu.CompilerParams(dimension_semantics=("parallel", "parallel"),
                                             vmem_limit_bytes=48 * 1024 * 1024),
        name="gdn_prompt",
    )(proj, proj, proj, proj, cw, cw, cw, gct, gct[..., None], bt[..., None], gdn_norm_g.reshape(1, GDN_DV))


def gdn_branch_prompt(proj, a, b, p):
    bsz, s, _ = proj.shape
    g = -jnp.exp(p['gdn_a_log']) * jax.nn.softplus(a + p['gdn_dt_bias'])
    gc = jnp.cumsum(g.reshape(bsz, s // GDN_CHUNK, GDN_CHUNK, GDN_HEADS), axis=2).reshape(bsz, s, GDN_HEADS)
    o, st = gdn_prompt(proj, p['conv_w'], gc, jax.nn.sigmoid(b), p['gdn_norm_g'])
    return o, proj[:, s - (CONV_K - 1):, :CONV_CH], st


def mla_project(q_a, kv_a, pos, p):
    bsz, s = q_a.shape[:2]
    cq = rms_norm(q_a, p['q_a_norm_g'])
    q = _matmul(cq.reshape(bsz * s, Q_LORA), p['w_qb'].reshape(Q_LORA, -1)).reshape(
        bsz, s, MLA_HEADS, NOPE + ROPE)
    q_nope = rms_norm(q[..., :NOPE], p['q_nope_norm_g'])
    q_rope = rotary(rms_norm(q[..., NOPE:], p['q_rope_norm_g']), pos)
    ckv = rms_norm(kv_a[..., :KV_LORA], p['kv_a_norm_g'])
    k_rope = rotary(rms_norm(kv_a[..., KV_LORA:], p['k_rope_norm_g'])[:, :, None, :], pos)[:, :, 0]
    return q_nope, q_rope, ckv, k_rope


def mla_expand(ckv, p):
    kv = jnp.einsum('bsc,chd->bshd', ckv, p['w_kvb'])
    return rms_norm(kv[..., :NOPE], p['k_nope_norm_g']), kv[..., NOPE:]


def mla_scores(q_nope, q_rope, k_nope, k_rope):
    s = jnp.einsum('bthd,bkhd->bhtk', q_nope, k_nope) + jnp.einsum('bthr,bkr->bhtk', q_rope, k_rope)
    return s.astype(jnp.float32) * ATTN_SCALE


_ATTN_TILE = 256


def _prompt_attn_body(qn_ref, qr_ref, kn_ref, kr_ref, v_ref, o_ref):
    t = _ATTN_TILE
    i = pl.program_id(1)
    row = lax.broadcasted_iota(jnp.int32, (t, t), 0)
    col = lax.broadcasted_iota(jnp.int32, (t, t), 1)
    for h in range(MLA_HEADS):
        qn = qn_ref[0, :, h * NOPE:(h + 1) * NOPE]
        qr = qr_ref[0, :, h * ROPE:(h + 1) * ROPE]

        def kv_block(j, carry, h=h, qn=qn, qr=qr):
            m, l, acc = carry
            rows = pl.ds(pl.multiple_of(j * t, t), t)
            kn = kn_ref[0, rows, h * NOPE:(h + 1) * NOPE]
            sc = lax.dot_general(qn, kn, _NT, preferred_element_type=jnp.float32)
            sc = sc + lax.dot_general(qr, kr_ref[0, rows, :], _NT, preferred_element_type=jnp.float32)
            sc = jnp.where(col + (j - i) * t <= row, sc * ATTN_SCALE, -jnp.inf)
            m_new = jnp.maximum(m, jnp.max(sc, axis=-1, keepdims=True))
            e = jnp.exp(sc - m_new)
            corr = jnp.exp(m - m_new)
            l = l * corr + jnp.sum(e, axis=-1, keepdims=True)
            acc = acc * corr + jnp.dot(e.astype(jnp.bfloat16), v_ref[0, rows, h * V_DIM:(h + 1) * V_DIM],
                                       preferred_element_type=jnp.float32)
            return m_new, l, acc

        init = (jnp.full((t, 1), -jnp.inf, jnp.float32), jnp.zeros((t, 1), jnp.float32),
                jnp.zeros((t, V_DIM), jnp.float32))
        m, l, acc = lax.fori_loop(0, i + 1, kv_block, init)
        o_ref[0, :, h * V_DIM:(h + 1) * V_DIM] = acc / l


def mla_prompt_flash(q_nope, q_rope, k_nope, k_rope, v):
    bsz, s, _ = q_nope.shape
    t = _ATTN_TILE
    h = MLA_HEADS
    assert s % t == 0
    q_tile = lambda w: pl.BlockSpec((1, t, w), lambda b, i: (b, i, 0))
    seq = lambda w: pl.BlockSpec((1, s, w), lambda b, i: (b, 0, 0))
    return pl.pallas_call(
        _prompt_attn_body,
        grid=(bsz, s // t),
        in_specs=[q_tile(h * NOPE), q_tile(h * ROPE), seq(h * NOPE), seq(ROPE), seq(h * V_DIM)],
        out_specs=q_tile(h * V_DIM),
        out_shape=jax.ShapeDtypeStruct((bsz, s, h * V_DIM), jnp.float32),
        compiler_params=pltpu.CompilerParams(dimension_semantics=("parallel", "arbitrary"),
                                             vmem_limit_bytes=48 * 1024 * 1024),
        name="mla_prompt_flash",
    )(q_nope, q_rope, k_nope, k_rope, v)


def mla_prompt_attention(q_nope, q_rope, ckv, k_rope, p):
    bsz, s = q_nope.shape[:2]
    bf16 = jnp.bfloat16
    kv = _matmul(ckv.reshape(bsz * s, KV_LORA), p['w_kvb'].reshape(KV_LORA, -1)).reshape(
        bsz, s, MLA_HEADS, NOPE + V_DIM)
    k_nope = rms_norm(kv[..., :NOPE], p['k_nope_norm_g'])
    return mla_prompt_flash(q_nope.reshape(bsz, s, -1).astype(bf16), q_rope.reshape(bsz, s, -1).astype(bf16),
                            k_nope.reshape(bsz, s, -1).astype(bf16), k_rope.astype(bf16),
                            kv[..., NOPE:].reshape(bsz, s, -1).astype(bf16))


_PAGES_PER_STEP = 8


def _head_block_mask(rows, cols, block):
    ri = lax.broadcasted_iota(jnp.int32, (rows, cols), 0)
    ci = lax.broadcasted_iota(jnp.int32, (rows, cols), 1)
    return (ci >= ri * block) & (ci < (ri + 1) * block)


def _sample_attn_body(pt_ref, qn_ref, qr_ref, ckv_ref, krn_ref, wkT_ref, wv_ref, gk_ref, *rest):
    npg = _PAGES_PER_STEP
    lat_refs, kr_refs = rest[:npg], rest[npg:2 * npg]
    o_ref, m_ref, l_ref, acc_ref, qabs_ref = rest[2 * npg:]
    j = pl.program_id(1)
    h = MLA_HEADS

    @pl.when(j == 0)
    def _():
        m_ref[...] = jnp.full_like(m_ref, -jnp.inf)
        l_ref[...] = jnp.zeros_like(l_ref)
        acc_ref[...] = jnp.zeros_like(acc_ref)
        qg = qn_ref[0] * gk_ref[...]
        qbd = jnp.where(_head_block_mask(h, h * NOPE, NOPE), jnp.concatenate([qg] * h, axis=1), 0.0)
        qabs_ref[...] = jnp.dot(qbd.astype(jnp.bfloat16), wkT_ref[...], preferred_element_type=jnp.float32)

    def merge(lat, kr, valid):
        t = lat.shape[0]
        lat_b = lat.astype(jnp.bfloat16)
        kt = lax.dot_general(wkT_ref[...], lat_b, _NT, preferred_element_type=jnp.float32)
        ssq = jnp.sum((kt * kt).reshape(h, NOPE, t), axis=1)
        r = lax.rsqrt(ssq * (1.0 / NOPE) + EPS)
        num = lax.dot_general(qabs_ref[...].astype(jnp.bfloat16), lat_b, _NT, preferred_element_type=jnp.float32)
        rope = lax.dot_general(qr_ref[0].astype(jnp.bfloat16), kr.astype(jnp.bfloat16), _NT,
                               preferred_element_type=jnp.float32)
        sc = (num * r + rope) * ATTN_SCALE
        if valid is not None:
            sc = jnp.where(valid, sc, -jnp.inf)
        m_old = m_ref[...]
        m_new = jnp.maximum(m_old, jnp.max(sc, axis=-1, keepdims=True))
        e = jnp.exp(sc - m_new)
        corr = jnp.exp(m_old - m_new)
        l_ref[...] = l_ref[...] * corr + jnp.sum(e, axis=-1, keepdims=True)
        acc_ref[...] = acc_ref[...] * corr + jnp.dot(e.astype(jnp.bfloat16), lat_b,
                                                     preferred_element_type=jnp.float32)
        m_ref[...] = m_new

    lat = jnp.concatenate([r[0] for r in lat_refs], axis=0)
    kr = jnp.concatenate([r[0] for r in kr_refs], axis=0)
    merge(lat, kr, None)

    @pl.when(j == pl.num_programs(1) - 1)
    def _():
        lat_n = jnp.broadcast_to(ckv_ref[0], (PAGE_SIZE, KV_LORA))
        kr_n = jnp.broadcast_to(krn_ref[0], (PAGE_SIZE, ROPE))
        merge(lat_n, kr_n, lax.broadcasted_iota(jnp.int32, (1, PAGE_SIZE), 1) == 0)
        o_lat = acc_ref[...] / l_ref[...]
        full = jnp.dot(o_lat.astype(jnp.bfloat16), wv_ref[...], preferred_element_type=jnp.float32)
        o_ref[0] = jnp.sum(jnp.where(_head_block_mask(h, MLA_V_W, V_DIM), full, 0.0), axis=0, keepdims=True)


def mla_sample_attention(q_nope, q_rope, ckv, k_rope, p, cache_latent, cache_krope, page_table):
    bsz, t = q_nope.shape[:2]
    assert t == 1, "one new token per sequence"
    n_pages = page_table.shape[1]
    npg = _PAGES_PER_STEP
    assert n_pages % npg == 0
    h = MLA_HEADS
    w_kvb = p['w_kvb']
    wkT = jnp.transpose(w_kvb[:, :, :NOPE], (1, 2, 0)).reshape(h * NOPE, KV_LORA).astype(jnp.bfloat16)
    wv = w_kvb[:, :, NOPE:].reshape(KV_LORA, h * V_DIM).astype(jnp.bfloat16)
    page = lambda pg, width: pl.BlockSpec((1, PAGE_SIZE, width), lambda b, j, pt: (pt[b, j * npg + pg], 0, 0))
    per_seq = lambda shape: pl.BlockSpec((1,) + shape, lambda b, j, pt: (b,) + (0,) * len(shape))
    whole = lambda shape: pl.BlockSpec(shape, lambda b, j, pt: (0,) * len(shape))
    grid_spec = pltpu.PrefetchScalarGridSpec(
        num_scalar_prefetch=1,
        grid=(bsz, n_pages // npg),
        in_specs=[per_seq((h, NOPE)), per_seq((h, ROPE)), per_seq((1, KV_LORA)), per_seq((1, ROPE)),
                  whole((h * NOPE, KV_LORA)), whole((KV_LORA, h * V_DIM)), whole((1, NOPE))]
                 + [page(pg, KV_LORA) for pg in range(npg)] + [page(pg, ROPE) for pg in range(npg)],
        out_specs=per_seq((1, MLA_V_W)),
        scratch_shapes=[pltpu.VMEM((h, 1), jnp.float32), pltpu.VMEM((h, 1), jnp.float32),
                        pltpu.VMEM((h, KV_LORA), jnp.float32), pltpu.VMEM((h, KV_LORA), jnp.float32)],
    )
    return pl.pallas_call(
        _sample_attn_body,
        grid_spec=grid_spec,
        out_shape=jax.ShapeDtypeStruct((bsz, 1, MLA_V_W), jnp.float32),
        compiler_params=pltpu.CompilerParams(dimension_semantics=("parallel", "arbitrary"),
                                             vmem_limit_bytes=48 * 1024 * 1024),
        name="mla_sample_attention",
    )(page_table, q_nope.reshape(bsz, h, NOPE), q_rope.reshape(bsz, h, ROPE), ckv, k_rope,
      wkT, wv, p['k_nope_norm_g'].reshape(1, NOPE), *([cache_latent] * npg), *([cache_krope] * npg))


_CAND_BLOCKS = ((0, 0, 8), (0, 8, 8), (1, 0, 8), (2, 0, 5), (3, 0, 4), (4, 0, 3), (5, 0, 2), (6, 0, 2), (7, 0, 2))


def _extract_top16(s, vals_ref, idx_ref):
    n = s.shape[0]
    row = lax.broadcasted_iota(jnp.int32, s.shape, 0).astype(jnp.float32)
    for k in range(PEER_TOPK):
        m = jnp.max(s, axis=0, keepdims=True)
        i = jnp.min(jnp.where(s == m, row, float(n)), axis=0, keepdims=True)
        vals_ref[k:k + 1, :] = m
        idx_ref[k:k + 1, :] = i
        s = jnp.where(row == i, -jnp.inf, s)


def _peer_retrieve_body(x_ref, wqT_ref, keys_ref, e_ref, g_ref, s1_ref, i1_ref, s2_ref, i2_ref, ts_ref):
    tm = x_ref.shape[0]
    qT = lax.dot_general(wqT_ref[...], x_ref[...], _NT, preferred_element_type=jnp.float32)
    for c, (s_ref, i_ref) in enumerate(((s1_ref, i1_ref), (s2_ref, i2_ref))):
        sT = jnp.dot(keys_ref[0, c], qT[c * N_KEYS:(c + 1) * N_KEYS].astype(jnp.bfloat16),
                     preferred_element_type=jnp.float32)
        _extract_top16(sT, s_ref, i_ref)
    sub = lax.broadcasted_iota(jnp.int32, (_SUBLANES, tm), 0).astype(jnp.float32)
    cs, js, es = [], [], []
    for a, b0, nv in _CAND_BLOCKS:
        c = s1_ref[a:a + 1, :] + s2_ref[b0:b0 + _SUBLANES, :]
        cs.append(jnp.where(sub < float(nv), c, -jnp.inf))
        js.append(sub + float(a * PEER_TOPK + b0))
        es.append(i1_ref[a:a + 1, :] * float(N_KEYS) + i2_ref[b0:b0 + _SUBLANES, :])
    cs.append(s1_ref[_SUBLANES:, :] + s2_ref[0:1, :])
    js.append((sub + float(_SUBLANES)) * float(PEER_TOPK))
    es.append(i1_ref[_SUBLANES:, :] * float(N_KEYS) + i2_ref[0:1, :])
    big = float(PEER_TOPK * PEER_TOPK)
    for k in range(PEER_TOPK):
        m = functools.reduce(jnp.maximum, cs)
        m = jnp.max(m, axis=0, keepdims=True)
        jm = functools.reduce(jnp.minimum, [jnp.where(c == m, j, big) for c, j in zip(cs, js)])
        jm = jnp.min(jm, axis=0, keepdims=True)
        hit = [j == jm for j in js]
        em = functools.reduce(jnp.maximum, [jnp.where(h, e, -1.0) for h, e in zip(hit, es)])
        ts_ref[k:k + 1, :] = m
        e_ref[0, k:k + 1, :] = jnp.max(em, axis=0, keepdims=True)
        cs = [jnp.where(h, -jnp.inf, c) for h, c in zip(hit, cs)]
    ts = ts_ref[...]
    ex = jnp.exp(ts - ts[0:1, :])
    g_ref[0] = ex / jnp.sum(ex, axis=0, keepdims=True)


def peer_retrieve(h2, wqT, keys, *, tm):
    t = h2.shape[0]
    out = jax.ShapeDtypeStruct((PEER_HEADS, PEER_TOPK, t), jnp.float32)
    scr = pltpu.VMEM((PEER_TOPK, tm), jnp.float32)
    return pl.pallas_call(
        _peer_retrieve_body,
        grid=(t // tm, PEER_HEADS),
        in_specs=[pl.BlockSpec((tm, D_MODEL), lambda i, h: (i, 0)),
                  pl.BlockSpec((PEER_DKEY, D_MODEL), lambda i, h: (h, 0)),
                  pl.BlockSpec((1, 2, N_KEYS, PEER_DKEY // 2), lambda i, h: (h, 0, 0, 0))],
        out_specs=[pl.BlockSpec((1, PEER_TOPK, tm), lambda i, h: (h, 0, i)),
                   pl.BlockSpec((1, PEER_TOPK, tm), lambda i, h: (h, 0, i))],
        out_shape=[out, out],
        scratch_shapes=[scr, scr, scr, scr, scr],
        compiler_params=pltpu.CompilerParams(dimension_semantics=("parallel", "arbitrary")),
        name="peer_retrieve",
    )(h2, wqT, keys)


_W_PITCH = 136


def _peer_wbuild_body(e_ref, g_ref, w_ref, i1_ref, i2_ref, gt_ref, s_ref):
    tw = w_ref.shape[0]
    hk = PEER_HEADS * PEER_TOPK
    eT = e_ref[...].reshape(hk, tw).T
    i1 = jnp.floor(eT * (1.0 / N_KEYS))
    i1_ref[...] = i1
    i2_ref[...] = eT - i1 * float(N_KEYS)
    gt_ref[...] = g_ref[...].reshape(hk, tw).T
    row = lax.broadcasted_iota(jnp.int32, (N_KEYS, hk), 0).astype(jnp.float32)

    def body(t, carry):
        a1 = jnp.where(row == i1_ref[pl.ds(t, 1), :], 1.0, 0.0).astype(jnp.bfloat16)
        a2 = jnp.where(row == i2_ref[pl.ds(t, 1), :], gt_ref[pl.ds(t, 1), :], 0.0).astype(jnp.bfloat16)
        wt = lax.dot_general(a1, a2, _NT, preferred_element_type=jnp.float32)
        s_ref[pl.ds(t, N_KEYS, stride=_W_PITCH), :] = wt
        return carry

    lax.fori_loop(0, tw, body, 0, unroll=_SUBLANES)
    for i in range(N_KEYS):
        w_ref[:, i * N_KEYS:(i + 1) * N_KEYS] = s_ref[i * _W_PITCH:i * _W_PITCH + tw, :].astype(jnp.bfloat16)


def peer_wbuild(e, g, *, tw=128):
    t = e.shape[-1]
    hk = PEER_HEADS * PEER_TOPK
    tok = pltpu.VMEM((tw, hk), jnp.float32)
    return pl.pallas_call(
        _peer_wbuild_body,
        grid=(t // tw,),
        in_specs=[pl.BlockSpec((PEER_HEADS, PEER_TOPK, tw), lambda i: (0, 0, i)),
                  pl.BlockSpec((PEER_HEADS, PEER_TOPK, tw), lambda i: (0, 0, i))],
        out_specs=pl.BlockSpec((tw, N_EXPERTS), lambda i: (i, 0)),
        out_shape=jax.ShapeDtypeStruct((t, N_EXPERTS), jnp.bfloat16),
        scratch_shapes=[tok, tok, tok, pltpu.VMEM((N_KEYS * _W_PITCH, N_KEYS), jnp.float32)],
        compiler_params=pltpu.CompilerParams(dimension_semantics=("parallel",),
                                             vmem_limit_bytes=40 * 1024 * 1024),
        name="peer_wbuild",
    )(e, g)


def _peer_dense_body(x_ref, u_ref, v_ref, w_ref, res_ref, g2_ref, o_ref, acc_ref):
    j = pl.program_id(1)

    @pl.when(j == 0)
    def _():
        acc_ref[...] = jnp.zeros_like(acc_ref)

    h = lax.dot_general(x_ref[...], u_ref[...], _NT, preferred_element_type=jnp.float32)
    act = 0.5 * h * (1.0 + lax.erf(h * (2.0 ** -0.5)))
    p = (act * w_ref[...].astype(jnp.float32)).astype(jnp.bfloat16)
    acc_ref[...] += jnp.dot(p, v_ref[...], preferred_element_type=jnp.float32)

    @pl.when(j == pl.num_programs(1) - 1)
    def _():
        o_ref[...] = res_ref[...] + g2_ref[0] * acc_ref[...]


def peer_dense(h2, u, v, w, res, g2, *, tm, te, tokens_per_gate_row):
    t = h2.shape[0]
    g_rows = g2.shape[1]
    tiles_per_gate = tokens_per_gate_row // tm
    return pl.pallas_call(
        _peer_dense_body,
        grid=(t // tm, N_EXPERTS // te),
        in_specs=[pl.BlockSpec((tm, D_MODEL), lambda i, j: (i, 0)),
                  pl.BlockSpec((te, D_MODEL), lambda i, j: (j, 0)),
                  pl.BlockSpec((te, D_MODEL), lambda i, j: (j, 0)),
                  pl.BlockSpec((tm, te), lambda i, j: (i, j)),
                  pl.BlockSpec((tm, D_MODEL), lambda i, j: (i, 0)),
                  pl.BlockSpec((1, g_rows, D_MODEL), lambda i, j: (i // tiles_per_gate, 0, 0))],
        out_specs=pl.BlockSpec((tm, D_MODEL), lambda i, j: (i, 0)),
        out_shape=jax.ShapeDtypeStruct((t, D_MODEL), jnp.float32),
        scratch_shapes=[pltpu.VMEM((tm, D_MODEL), jnp.float32)],
        compiler_params=pltpu.CompilerParams(dimension_semantics=("parallel", "arbitrary"),
                                             vmem_limit_bytes=48 * 1024 * 1024),
        name="peer_dense",
    )(h2, u, v, w, res, g2)


def peer_residual(x, h2, g2, p):
    bsz, s, d = x.shape
    n_tok = bsz * s
    wqT = p['peer_wq'].T.astype(jnp.bfloat16)
    keys = p['peer_keys'].astype(jnp.bfloat16)
    u = p['peer_u'].astype(jnp.bfloat16)
    v = p['peer_v'].astype(jnp.bfloat16)
    h2 = h2.reshape(n_tok, d).astype(jnp.bfloat16)
    res = x.reshape(n_tok, d)
    if s % 1024 == 0:
        tm_r, tm_d, te, per_gate = 256, 1024, 512, s
    else:
        pad = _round_up(n_tok, LANES) - n_tok
        h2 = jnp.pad(h2, ((0, pad), (0, 0)))
        res = jnp.pad(res, ((0, pad), (0, 0)))
        g2 = jnp.pad(jnp.broadcast_to(g2, (bsz, s, d)).reshape(1, n_tok, d), ((0, 0), (0, pad), (0, 0)))
        tm_r = tm_d = per_gate = n_tok + pad
        te = 512
    e, g = peer_retrieve(h2, wqT, keys, tm=tm_r)
    w = peer_wbuild(e, g)
    out = peer_dense(h2, u, v, w, res, g2, tm=tm_d, te=te, tokens_per_gate_row=per_gate)
    return out[:n_tok].reshape(bsz, s, d)


def trunk_layer(x, c, pos, conv_buf, gdn_state, attend, p):
    bsz, s, d = x.shape
    mod = jax.nn.silu(c) @ p['w_ada'] + p['b_ada']
    sh1, sc1, g1, sh2, sc2, g2 = jnp.split(mod[:, None, :], 6, axis=-1)
    h = rms_norm(x, p['norm1_g']) * (1 + sc1) + sh1
    proj = _matmul(h.reshape(bsz * s, d), p['w_in']).reshape(bsz, s, -1)
    qkv, z, a, b, q_a, kv_a, gates = jnp.split(proj, IN_OFFSETS, axis=-1)
    if conv_buf is None:
        o_a, new_conv, new_s = gdn_branch_prompt(proj, a, b, p)
    else:
        o_a, new_conv, new_s = gdn_branch(qkv, z, a, b, conv_buf, gdn_state, p)
    q_nope, q_rope, ckv, k_rope = mla_project(q_a, kv_a, pos, p)
    o_b = attend(q_nope, q_rope, ckv, k_rope, p)
    gate_a, gate_b = jnp.split(jax.nn.sigmoid(gates), 2, axis=-1)
    ma = _matmul(o_a.reshape(bsz * s, -1), p['w_branch_a']).reshape(bsz, s, d)
    mb = _matmul(o_b.reshape(bsz * s, -1), p['w_branch_b']).reshape(bsz, s, d)
    merged = gate_a * ma + gate_b * mb
    x = x + g1 * _matmul(merged.reshape(bsz * s, d), p['w_out']).reshape(bsz, s, d)
    h2 = rms_norm(x, p['norm2_g']) * (1 + sc2) + sh2
    x = peer_residual(x, h2, g2, p)
    return x, ckv, k_rope, new_s, new_conv


def kernel(x_prompt, x_sample, c_prompt, c_sample, cache_latent, cache_krope, state_gdn, state_conv,
           page_table, w_ada, b_ada, norm1_g, norm2_g, w_in, conv_w, gdn_a_log, gdn_dt_bias, gdn_norm_g,
           q_a_norm_g, w_qb, kv_a_norm_g, w_kvb, q_nope_norm_g, q_rope_norm_g, k_nope_norm_g, k_rope_norm_g,
           w_branch_a, w_branch_b, w_out, peer_wq, peer_keys, peer_u, peer_v):
    bsz_p, seq_p = x_prompt.shape[:2]
    pos_p = jnp.arange(seq_p, dtype=jnp.int32)
    past = page_table.shape[1] * PAGE_SIZE
    pos_s = past + jnp.arange(x_sample.shape[1], dtype=jnp.int32)
    l = 0
    p = {
        'w_ada': w_ada[l], 'b_ada': b_ada[l], 'norm1_g': norm1_g[l], 'norm2_g': norm2_g[l],
        'w_in': w_in[l], 'conv_w': conv_w[l], 'gdn_a_log': gdn_a_log[l], 'gdn_dt_bias': gdn_dt_bias[l],
        'gdn_norm_g': gdn_norm_g[l], 'q_a_norm_g': q_a_norm_g[l], 'w_qb': w_qb[l],
        'kv_a_norm_g': kv_a_norm_g[l], 'w_kvb': w_kvb[l], 'q_nope_norm_g': q_nope_norm_g[l],
        'q_rope_norm_g': q_rope_norm_g[l], 'k_nope_norm_g': k_nope_norm_g[l],
        'k_rope_norm_g': k_rope_norm_g[l], 'w_branch_a': w_branch_a[l], 'w_branch_b': w_branch_b[l],
        'w_out': w_out[l], 'peer_wq': peer_wq[l], 'peer_keys': peer_keys[l], 'peer_u': peer_u[l],
        'peer_v': peer_v[l],
    }
    xp, lat_p, kr_p, g_p, cv_p = trunk_layer(x_prompt, c_prompt, pos_p, None, None, mla_prompt_attention, p)
    attend = functools.partial(mla_sample_attention, cache_latent=cache_latent[l],
                               cache_krope=cache_krope[l], page_table=page_table)
    xs, lat_s, kr_s, g_s, cv_s = trunk_layer(x_sample, c_sample, pos_s, state_conv[l], state_gdn[l], attend, p)
    st = lambda t: t[None]
    return (xp, xs, st(lat_p), st(kr_p), st(g_p), st(cv_p), st(lat_s), st(kr_s), st(g_s), st(cv_s))
```
